```python
import math
import jax, jax.numpy as jnp
from jax import lax
import numpy as np

D_MODEL = 1024
BATCH = 8
SEQ = 2048
DEPTH = 4
DEC_BATCH = 32
DEC_SEQ = 1
PAST_LEN = 8192
PAGE_SIZE = 128

HEAD_DIM = 64
N_FOX_HEADS = D_MODEL // 128
N_DIFF_HEADS = D_MODEL // 256
FOX_WIDTH = N_FOX_HEADS * HEAD_DIM
DIFF_WIDTH = N_DIFF_HEADS * 2 * HEAD_DIM
NUM_BUCKETS = 32
MAX_DISTANCE = 128
Q_BLOCK = 128
RMS_EPS = 1e-6
FORGET_BIAS_INIT = 4.0
NEG_INF = -1e30
SPLIT_SIZES = (FOX_WIDTH, FOX_WIDTH, FOX_WIDTH, N_FOX_HEADS, FOX_WIDTH,
               DIFF_WIDTH, DIFF_WIDTH, DIFF_WIDTH, DIFF_WIDTH, D_MODEL, D_MODEL)
IN_WIDTH = 5 * FOX_WIDTH - FOX_WIDTH + N_FOX_HEADS + 4 * DIFF_WIDTH + 2 * D_MODEL

kernel_name = 'hybrid_fox_diff_gated_decoder_step'


def rmsnorm(x, g):
    xf = x.astype(jnp.float32)
    y = xf * lax.rsqrt(jnp.mean(xf * xf, axis=-1, keepdims=True) + RMS_EPS)
    return (y * g.astype(jnp.float32)).astype(x.dtype)


def lambda_init(layer):
    return 0.8 - 0.6 * math.exp(-0.3 * layer)


def t5_bucket(rel):
    n = jnp.maximum(rel, 0)
    max_exact = NUM_BUCKETS // 2
    nf = jnp.maximum(n, 1).astype(jnp.float32)
    large = max_exact + (jnp.log(nf / max_exact) / math.log(MAX_DISTANCE / max_exact)
                         * (NUM_BUCKETS - max_exact)).astype(jnp.int32)
    large = jnp.minimum(large, NUM_BUCKETS - 1)
    return jnp.where(n < max_exact, n, large)


def causal_mask(q_pos, k_pos):
    return k_pos[None, :] <= q_pos[:, None]


def softmax_segments(logits_list):
    p = jax.nn.softmax(jnp.concatenate(logits_list, axis=-1), axis=-1)
    out, start = [], 0
    for lg in logits_list:
        n = lg.shape[-1]
        out.append(p[..., start:start + n])
        start += n
    return out


def fox_attention(q, c_q, q_pos, segs):
    scale = HEAD_DIM ** -0.5
    cq = jnp.transpose(c_q, (0, 2, 1))[..., :, None]
    logits = []
    for k, v, c_k, k_pos in segs:
        s = jnp.einsum('bqhd,bkhd->bhqk', q, k).astype(jnp.float32) * scale
        s = s + cq - jnp.transpose(c_k, (0, 2, 1))[..., None, :]
        logits.append(jnp.where(causal_mask(q_pos, k_pos), s, NEG_INF))
    probs = softmax_segments(logits)
    return sum(jnp.einsum('bhqk,bkhd->bqhd', p.astype(seg[1].dtype), seg[1])
               for p, seg in zip(probs, segs))


def diff_attention(q, q_pos, segs, rel_bias, lam):
    scale = HEAD_DIM ** -0.5
    logits = []
    for k, v, k_pos in segs:
        s = jnp.einsum('bqhcd,bkhcd->bhcqk', q, k).astype(jnp.float32) * scale
        bias = jnp.transpose(rel_bias[t5_bucket(q_pos[:, None] - k_pos[None, :])], (2, 0, 1))
        s = s + bias.astype(jnp.float32)[None, :, None]
        logits.append(jnp.where(causal_mask(q_pos, k_pos), s, NEG_INF))
    probs = softmax_segments(logits)
    return sum(jnp.einsum('bhqk,bkhe->bqhe', (p[:, :, 0] - lam * p[:, :, 1]).astype(seg[1].dtype), seg[1])
               for p, seg in zip(probs, segs))


def in_projection(h, w_in_l, b_f_l):
    z = h @ w_in_l
    idx = [int(i) for i in np.cumsum(SPLIT_SIZES)[:-1]]
    fq, fk, fv, ff, fz, dq, dk, dv, dz, ga, gb = jnp.split(z, idx, axis=-1)
    B, S = h.shape[0], h.shape[1]
    logf = jax.nn.log_sigmoid((ff + b_f_l).astype(jnp.float32))
    fox = (fq.reshape(B, S, N_FOX_HEADS, HEAD_DIM), fk.reshape(B, S, N_FOX_HEADS, HEAD_DIM),
           fv.reshape(B, S, N_FOX_HEADS, HEAD_DIM), logf, fz)
    diff = (dq.reshape(B, S, N_DIFF_HEADS, 2, HEAD_DIM), dk.reshape(B, S, N_DIFF_HEADS, 2, HEAD_DIM),
            dv.reshape(B, S, N_DIFF_HEADS, 2 * HEAD_DIM), dz)
    return fox, diff, ga, gb


def diff_head_norm(o, g, lam_init_l):
    return rmsnorm(o, g) * (1.0 - lam_init_l)


def merge_output(o_fox, z_fox, o_diff, z_diff, ga, gb, w_bf, w_bd, w_o):
    B, S = o_fox.shape[0], o_fox.shape[1]
    y_fox = (o_fox.reshape(B, S, FOX_WIDTH) * jax.nn.silu(z_fox)) @ w_bf
    y_diff = (o_diff.reshape(B, S, DIFF_WIDTH) * jax.nn.silu(z_diff)) @ w_bd
    m = jax.nn.sigmoid(ga) * y_fox + jax.nn.sigmoid(gb) * y_diff
    return m @ w_o


def setup_inputs(seed: int = 0) -> dict:
    key = jax.random.key(seed)
    ks = jax.random.split(key, 24)
    f32 = jnp.float32
    n_pages = PAST_LEN // PAGE_SIZE
    n_used = DEC_BATCH * n_pages
    n_pool = n_used + n_used // 4
    nrm = jax.random.normal
    return {
        'x_prompt': nrm(ks[0], (BATCH, SEQ, D_MODEL), f32),
        'x_sample': nrm(ks[1], (DEC_BATCH, DEC_SEQ, D_MODEL), f32),
        'cache_fox_k': nrm(ks[2], (DEPTH, n_pool, PAGE_SIZE, N_FOX_HEADS, HEAD_DIM), f32),
        'cache_fox_v': nrm(ks[3], (DEPTH, n_pool, PAGE_SIZE, N_FOX_HEADS, HEAD_DIM), f32),
        'cache_fox_logf': jax.nn.log_sigmoid(FORGET_BIAS_INIT + nrm(ks[4], (DEPTH, n_pool, PAGE_SIZE, N_FOX_HEADS), f32)),
        'cache_diff_k': nrm(ks[5], (DEPTH, n_pool, PAGE_SIZE, N_DIFF_HEADS, 2, HEAD_DIM), f32),
        'cache_diff_v': nrm(ks[6], (DEPTH, n_pool, PAGE_SIZE, N_DIFF_HEADS, 2 * HEAD_DIM), f32),
        'page_table': jax.random.permutation(ks[7], n_pool)[:n_used].reshape(DEC_BATCH, n_pages).astype(jnp.int32),
        'norm_g': 1.0 + 0.02 * nrm(ks[8], (DEPTH, D_MODEL), f32),
        'w_in': nrm(ks[9], (DEPTH, D_MODEL, IN_WIDTH), f32) * D_MODEL ** -0.5,
        'b_forget': FORGET_BIAS_INIT + 0.1 * nrm(ks[10], (DEPTH, N_FOX_HEADS), f32),
        'lambda_q1': 0.1 * nrm(ks[11], (DEPTH, HEAD_DIM), f32),
        'lambda_k1': 0.1 * nrm(ks[12], (DEPTH, HEAD_DIM), f32),
        'lambda_q2': 0.1 * nrm(ks[13], (DEPTH, HEAD_DIM), f32),
        'lambda_k2': 0.1 * nrm(ks[14], (DEPTH, HEAD_DIM), f32),
        'diff_subln_g': 1.0 + 0.02 * nrm(ks[15], (DEPTH, 2 * HEAD_DIM), f32),
        'w_branch_fox': nrm(ks[16], (DEPTH, FOX_WIDTH, D_MODEL), f32) * FOX_WIDTH ** -0.5,
        'w_branch_diff': nrm(ks[17], (DEPTH, DIFF_WIDTH, D_MODEL), f32) * DIFF_WIDTH ** -0.5,
        'w_out': nrm(ks[18], (DEPTH, D_MODEL, D_MODEL), f32) * D_MODEL ** -0.5,
        'rel_bias': 0.5 * nrm(ks[19], (NUM_BUCKETS, N_DIFF_HEADS), f32),
        'final_norm_g': 1.0 + 0.02 * nrm(ks[20], (D_MODEL,), f32),
    }


def reference(x_prompt, x_sample, cache_fox_k, cache_fox_v, cache_fox_logf, cache_diff_k, cache_diff_v,
              page_table, norm_g, w_in, b_forget, lambda_q1, lambda_k1, lambda_q2, lambda_k2,
              diff_subln_g, w_branch_fox, w_branch_diff, w_out, rel_bias, final_norm_g):
    f32 = jnp.float32
    Sp = x_prompt.shape[1]
    Bs, Ss = x_sample.shape[0], x_sample.shape[1]
    P = page_table.shape[1] * cache_fox_k.shape[2]
    pos_p = jnp.arange(Sp, dtype=jnp.int32)
    pos_past = jnp.arange(P, dtype=jnp.int32)
    pos_s = P + jnp.arange(Ss, dtype=jnp.int32)
    pk, pv, plf, pdk, pdv = [], [], [], [], []
    sk_l, sv_l, slf_l, sdk_l, sdv_l = [], [], [], [], []
    xp, xs = x_prompt, x_sample
    for l in range(DEPTH):
        lam_init_l = lambda_init(l)
        lam = (jnp.exp(jnp.sum(lambda_q1[l].astype(f32) * lambda_k1[l].astype(f32)))
               - jnp.exp(jnp.sum(lambda_q2[l].astype(f32) * lambda_k2[l].astype(f32))) + lam_init_l)

        hp = rmsnorm(xp, norm_g[l])
        (fq, fk, fv, flf, fz), (dq, dk, dv, dz), ga, gb = in_projection(hp, w_in[l], b_forget[l])
        fc = jnp.cumsum(flf, axis=1)
        fox_blocks, diff_blocks = [], []
        for s0 in range(0, Sp, Q_BLOCK):
            s1 = s0 + Q_BLOCK
            fox_blocks.append(fox_attention(fq[:, s0:s1], fc[:, s0:s1], pos_p[s0:s1],
                                            [(fk[:, :s1], fv[:, :s1], fc[:, :s1], pos_p[:s1])]))
            diff_blocks.append(diff_attention(dq[:, s0:s1], pos_p[s0:s1],
                                              [(dk[:, :s1], dv[:, :s1], pos_p[:s1])], rel_bias, lam))
        o_fox = jnp.concatenate(fox_blocks, axis=1)
        o_diff = diff_head_norm(jnp.concatenate(diff_blocks, axis=1), diff_subln_g[l], lam_init_l)
        xp = xp + merge_output(o_fox, fz, o_diff, dz, ga, gb, w_branch_fox[l], w_branch_diff[l], w_out[l])
        pk.append(fk); pv.append(fv); plf.append(flf); pdk.append(dk); pdv.append(dv)

        hs = rmsnorm(xs, norm_g[l])
        (sq, sk, sv, slf, sz), (sdq, sdk, sdv, sdz), sga, sgb = in_projection(hs, w_in[l], b_forget[l])
        past_fk = cache_fox_k[l, page_table].reshape(Bs, P, N_FOX_HEADS, HEAD_DIM)
        past_fv = cache_fox_v[l, page_table].reshape(Bs, P, N_FOX_HEADS, HEAD_DIM)
        past_lf = cache_fox_logf[l, page_table].reshape(Bs, P, N_FOX_HEADS).astype(f32)
        c_all = jnp.cumsum(jnp.concatenate([past_lf, slf], axis=1), axis=1)
        c_past, c_new = c_all[:, :P], c_all[:, P:]
        o_fox_s = fox_attention(sq, c_new, pos_s,
                                [(past_fk, past_fv, c_past, pos_past), (sk, sv, c_new, pos_s)])
        past_dk = cache_diff_k[l, page_table].reshape(Bs, P, N_DIFF_HEADS, 2, HEAD_DIM)
        past_dv = cache_diff_v[l, page_table].reshape(Bs, P, N_DIFF_HEADS, 2 * HEAD_DIM)
        o_diff_s = diff_head_norm(diff_attention(sdq, pos_s, [(past_dk, past_dv, pos_past), (sdk, sdv, pos_s)],
                                                 rel_bias, lam), diff_subln_g[l], lam_init_l)
        xs = xs + merge_output(o_fox_s, sz, o_diff_s, sdz, sga, sgb, w_branch_fox[l], w_branch_diff[l], w_out[l])
        sk_l.append(sk); sv_l.append(sv); slf_l.append(slf); sdk_l.append(sdk); sdv_l.append(sdv)

    y_prompt = rmsnorm(xp, final_norm_g)
    y_sample = rmsnorm(xs, final_norm_g)
    return (y_prompt, y_sample,
            jnp.stack(pk), jnp.stack(pv), jnp.stack(plf), jnp.stack(pdk), jnp.stack(pdv),
            jnp.stack(sk_l), jnp.stack(sv_l), jnp.stack(slf_l), jnp.stack(sdk_l), jnp.stack(sdv_l))
```

```python
import functools
import math

import jax
import jax.numpy as jnp
from jax import lax
from jax.experimental import pallas as pl
from jax.experimental.pallas import tpu as pltpu

HEAD_DIM = 64
LANES = 128
FOX_WIDTH = 512
DIFF_WIDTH = 512
NUM_BUCKETS = 32
MAX_DISTANCE = 128
RMS_EPS = 1e-6
NEG_INF = -1e30
SCALE = HEAD_DIM ** -0.5
VMEM_LIMIT = 56 * 1024 * 1024

F32 = jnp.float32
BF16 = jnp.bfloat16

_NT = (((1,), (1,)), ((), ()))


def _lambda_init(layer):
    return 0.8 - 0.6 * math.exp(-0.3 * layer)


def _sigmoid(x):
    return 1.0 / (1.0 + jnp.exp(-x))


def _log_sigmoid(x):
    return jnp.minimum(x, 0.0) - jnp.log1p(jnp.exp(-jnp.abs(x)))


def _split3(x):
    h1 = x.astype(BF16)
    r1 = x - h1.astype(F32)
    h2 = r1.astype(BF16)
    r2 = r1 - h2.astype(F32)
    return h1, h2, r2.astype(BF16)


def _t5_bucket(rel):
    n = jnp.maximum(rel, 0)
    max_exact = NUM_BUCKETS // 2
    nf = jnp.maximum(n, 1).astype(F32)
    large = max_exact + (jnp.log(nf / max_exact) / math.log(MAX_DISTANCE / max_exact)
                         * (NUM_BUCKETS - max_exact)).astype(jnp.int32)
    large = jnp.minimum(large, NUM_BUCKETS - 1)
    return jnp.where(n < max_exact, n, large)


def _lam_value(lq1, lk1, lq2, lk2, lam_init):
    a = jnp.sum(lq1 * lk1, axis=1, keepdims=True)
    b = jnp.sum(lq2 * lk2, axis=1, keepdims=True)
    return jnp.exp(a) - jnp.exp(b) + lam_init


def _inproj_kernel(x_ref, g_ref, w_ref, wf_ref, bf_ref,
                   fq_ref, fk_ref, fkb_ref, fv_ref, fvb_ref, fz_ref,
                   dq_ref, dk_ref, dkb_ref, dv_ref, dvb_ref, dz_ref,
                   ga_ref, gb_ref, lf_ref, *rest, tiles_per_seq, with_cumsum):
    x = x_ref[...]
    y = x * lax.rsqrt(jnp.mean(x * x, axis=-1, keepdims=True) + RMS_EPS)
    hb = (y * g_ref[...]).astype(BF16)

    def proj(j, width):
        return jnp.dot(hb, w_ref[:, j:j + width], preferred_element_type=F32)

    fq_ref[...] = (proj(0, 512) * SCALE).astype(BF16)
    k = proj(512, 512)
    fk_ref[...] = k
    fkb_ref[...] = k.astype(BF16)
    v = proj(1024, 512)
    fv_ref[...] = v
    fvb_ref[...] = v.astype(BF16)
    fz_ref[...] = proj(1536, 512)
    dq_ref[...] = (proj(2048, 512) * SCALE).astype(BF16)
    k = proj(2560, 512)
    dk_ref[...] = k
    dkb_ref[...] = k.astype(BF16)
    v = proj(3072, 512)
    dv_ref[...] = v
    dvb_ref[...] = v.astype(BF16)
    dz_ref[...] = proj(3584, 512)
    ga_ref[...] = proj(4096, 1024)
    gb_ref[...] = proj(5120, 1024)

    ff = jnp.dot(hb, wf_ref[...], preferred_element_type=F32) + bf_ref[...]
    lf = _log_sigmoid(ff)
    lf_ref[...] = lf[:, :8]

    if with_cumsum:
        c_ref, ct_ref, carry_ref = rest
        tm = x.shape[0]
        i = pl.program_id(0)

        @pl.when(i % tiles_per_seq == 0)
        def _():
            carry_ref[...] = jnp.zeros_like(carry_ref)

        row = lax.broadcasted_iota(jnp.int32, (tm, tm), 0)
        col = lax.broadcasted_iota(jnp.int32, (tm, tm), 1)
        tri = jnp.where(col <= row, 1.0, 0.0).astype(BF16)
        h1, h2, h3 = _split3(lf)
        c = (jnp.dot(tri, h1, preferred_element_type=F32)
             + jnp.dot(tri, h2, preferred_element_type=F32)
             + jnp.dot(tri, h3, preferred_element_type=F32)) + carry_ref[...]
        carry_ref[...] = c[tm - 1:tm, :]
        c_ref[...] = c[:, :8]
        ct_ref[...] = c.T[:8, :]


def _inproj(x2d, g, w_main, w_f, b_f, *, tm, rows_per_seq, with_cumsum):
    m, d = x2d.shape
    n_tiles = m // tm
    row_spec = lambda width: pl.BlockSpec((tm, width), lambda i: (i, 0))
    const = lambda shape: pl.BlockSpec(shape, lambda i: (0, 0), pipeline_mode=pl.Buffered(1))
    out_shape = [
        jax.ShapeDtypeStruct((m, 512), BF16),
        jax.ShapeDtypeStruct((m, 512), F32),
        jax.ShapeDtypeStruct((m, 512), BF16),
        jax.ShapeDtypeStruct((m, 512), F32),
        jax.ShapeDtypeStruct((m, 512), BF16),
        jax.ShapeDtypeStruct((m, 512), F32),
        jax.ShapeDtypeStruct((m, 512), BF16),
        jax.ShapeDtypeStruct((m, 512), F32),
        jax.ShapeDtypeStruct((m, 512), BF16),
        jax.ShapeDtypeStruct((m, 512), F32),
        jax.ShapeDtypeStruct((m, 512), BF16),
        jax.ShapeDtypeStruct((m, 512), F32),
        jax.ShapeDtypeStruct((m, 1024), F32),
        jax.ShapeDtypeStruct((m, 1024), F32),
        jax.ShapeDtypeStruct((m, 8), F32),
    ]
    out_specs = [row_spec(s.shape[1]) for s in out_shape]
    scratch = []
    if with_cumsum:
        out_shape += [jax.ShapeDtypeStruct((m, 8), F32), jax.ShapeDtypeStruct((8, m), F32)]
        out_specs += [row_spec(8), pl.BlockSpec((8, tm), lambda i: (0, i))]
        scratch = [pltpu.VMEM((1, LANES), F32)]
    return pl.pallas_call(
        functools.partial(_inproj_kernel, tiles_per_seq=max(rows_per_seq // tm, 1), with_cumsum=with_cumsum),
        grid=(n_tiles,),
        in_specs=[row_spec(d), const((1, d)), const(w_main.shape), const(w_f.shape), const((1, LANES))],
        out_specs=out_specs,
        out_shape=out_shape,
        scratch_shapes=scratch,
        compiler_params=pltpu.CompilerParams(dimension_semantics=("arbitrary",), vmem_limit_bytes=VMEM_LIMIT),
        name="inproj_prompt" if with_cumsum else "inproj_sample",
    )(x2d, g, w_main, w_f, b_f)


def _bias_lookup(rb_ref, bucket, head):
    out = jnp.zeros(bucket.shape, F32)
    for b in range(NUM_BUCKETS):
        out = jnp.where(bucket == b, rb_ref[b, head], out)
    return out


def _prompt_bias_kernel(rb_ref, out_ref, *, tq, tk):
    h = pl.program_id(0)
    r = lax.broadcasted_iota(jnp.int32, (tq, tk), 0)
    c = lax.broadcasted_iota(jnp.int32, (tq, tk), 1)
    for t in range(3):
        bucket = _t5_bucket(r - c + t * tk)
        out_ref[0, t] = _bias_lookup(rb_ref, bucket, h)


def _prompt_bias(rel_bias, tq, tk):
    return pl.pallas_call(
        functools.partial(_prompt_bias_kernel, tq=tq, tk=tk),
        grid=(4,),
        in_specs=[pl.BlockSpec(memory_space=pltpu.SMEM)],
        out_specs=pl.BlockSpec((1, 3, tq, tk), lambda h: (h, 0, 0, 0)),
        out_shape=jax.ShapeDtypeStruct((4, 3, tq, tk), F32),
        name="prompt_bias",
    )(rel_bias)


def _decode_bias_kernel(rb_ref, out_ref, self_ref, *, past_len, page):
    p = pl.program_id(0)
    pos = p * page + lax.broadcasted_iota(jnp.int32, (8, page), 1)
    row = lax.broadcasted_iota(jnp.int32, (8, page), 0)
    bucket = _t5_bucket(past_len - pos)
    bucket0 = _t5_bucket(jnp.zeros((8, page), jnp.int32))
    acc = jnp.zeros((8, page), F32)
    acc0 = jnp.zeros((8, page), F32)
    for hh in range(4):
        sel = (row // 2) == hh
        acc = jnp.where(sel, _bias_lookup(rb_ref, bucket, hh), acc)
        acc0 = jnp.where(sel, _bias_lookup(rb_ref, bucket0, hh), acc0)
    out_ref[0] = acc
    self_ref[...] = acc0


def _decode_bias(rel_bias, past_len, page):
    n_pages = past_len // page
    return pl.pallas_call(
        functools.partial(_decode_bias_kernel, past_len=past_len, page=page),
        grid=(n_pages,),
        in_specs=[pl.BlockSpec(memory_space=pltpu.SMEM)],
        out_specs=[pl.BlockSpec((1, 8, page), lambda p: (p, 0, 0)), pl.BlockSpec((8, page), lambda p: (0, 0))],
        out_shape=[jax.ShapeDtypeStruct((n_pages, 8, page), F32), jax.ShapeDtypeStruct((8, page), F32)],
        name="decode_bias",
    )(rel_bias)


def _attn_chunk(j, masked, q0, q1, k_ref, v_ref, add_terms, row_terms, m_sc, l_sc, acc_sc, tk):
    start = pl.multiple_of(j * tk, tk)
    ks = k_ref[0, pl.ds(start, tk), :]
    vs = v_ref[0, pl.ds(start, tk), :]
    tq = q0.shape[0]
    if masked:
        r = lax.broadcasted_iota(jnp.int32, (tq, tk), 0)
        col = lax.broadcasted_iota(jnp.int32, (tq, tk), 1)
        keep = col <= r
    for c, qc in enumerate((q0, q1)):
        t = lax.dot_general(qc, ks, _NT, preferred_element_type=F32) + add_terms(c)
        if masked:
            t = jnp.where(keep, t, NEG_INF)
        rt = row_terms[c]
        mrow = jnp.max(t, axis=1, keepdims=True)
        if rt is not None:
            mrow = mrow + rt
        m_old = m_sc[c]
        m_new = jnp.maximum(m_old, mrow)
        shift = m_new if rt is None else m_new - rt
        p = jnp.exp(t - shift)
        alpha = jnp.exp(m_old - m_new)
        l_sc[c] = alpha * l_sc[c] + jnp.sum(p, axis=1, keepdims=True)
        acc_sc[c] = alpha * acc_sc[c] + jnp.dot(p.astype(BF16), vs, preferred_element_type=F32)
        m_sc[c] = m_new


def _attn_init(m_sc, l_sc, acc_sc):
    m_sc[...] = jnp.full(m_sc.shape, -jnp.inf, F32)
    l_sc[...] = jnp.zeros_like(l_sc)
    acc_sc[...] = jnp.zeros_like(acc_sc)


def _split_q(q_ref):
    q = q_ref[0]
    lane = lax.broadcasted_iota(jnp.int32, q.shape, 1)
    zero = jnp.zeros_like(q)
    return jnp.where(lane < HEAD_DIM, q, zero), jnp.where(lane >= HEAD_DIM, q, zero)


def _fox_attn_kernel(q_ref, k_ref, v_ref, cq_ref, ck_ref, o_ref, m_sc, l_sc, acc_sc, *, tk):
    qi = pl.program_id(2)
    q0, q1 = _split_q(q_ref)
    cq = cq_ref[0, 0]
    row_terms = (cq[:, 0:1], cq[:, 1:2])
    _attn_init(m_sc, l_sc, acc_sc)

    def step(j, masked):
        ck = ck_ref[0, 0, j]
        _attn_chunk(j, masked, q0, q1, k_ref, v_ref, lambda c: -ck[c:c + 1, :], row_terms,
                    m_sc, l_sc, acc_sc, tk)

    def body(j, carry):
        step(j, False)
        return carry

    lax.fori_loop(0, qi, body, 0)
    step(qi, True)
    lane = lax.broadcasted_iota(jnp.int32, acc_sc.shape[1:], 1)
    o_ref[0] = jnp.where(lane < HEAD_DIM, acc_sc[0] / l_sc[0], acc_sc[1] / l_sc[1])


def _diff_attn_kernel(q_ref, k_ref, v_ref, bias_ref, lq1_ref, lk1_ref, lq2_ref, lk2_ref, g_ref,
                      o_ref, m_sc, l_sc, acc_sc, *, tk, lam_init):
    qi = pl.program_id(2)
    q0, q1 = _split_q(q_ref)
    _attn_init(m_sc, l_sc, acc_sc)

    def step(j, tile, masked):
        _attn_chunk(j, masked, q0, q1, k_ref, v_ref, lambda c: bias_ref[0, tile], (None, None),
                    m_sc, l_sc, acc_sc, tk)

    def body(j, carry):
        step(j, 2, False)
        return carry

    lax.fori_loop(0, qi - 1, body, 0)

    @pl.when(qi >= 1)
    def _():
        step(qi - 1, 1, False)

    step(qi, 0, True)
    lam = _lam_value(lq1_ref[...], lk1_ref[...], lq2_ref[...], lk2_ref[...], lam_init)
    o = acc_sc[0] / l_sc[0] - lam * (acc_sc[1] / l_sc[1])
    y = o * lax.rsqrt(jnp.mean(o * o, axis=-1, keepdims=True) + RMS_EPS)
    o_ref[0] = (y * g_ref[...]) * (1.0 - lam_init)


def _attn_common(b, s, tq):
    n_units = FOX_WIDTH // LANES
    grid = (b, n_units, s // tq)
    q_spec = pl.BlockSpec((1, tq, LANES), lambda bi, u, qi: (bi, qi, u))
    kv_spec = pl.BlockSpec((1, s, LANES), lambda bi, u, qi: (bi, 0, u))
    scratch = [pltpu.VMEM((2, tq, 1), F32), pltpu.VMEM((2, tq, 1), F32), pltpu.VMEM((2, tq, LANES), F32)]
    params = pltpu.CompilerParams(dimension_semantics=("arbitrary", "arbitrary", "arbitrary"),
                                  vmem_limit_bytes=VMEM_LIMIT)
    return grid, q_spec, kv_spec, scratch, params


def _fox_attention(q, kb, vb, cq, ck, *, tq):
    b, s, _ = q.shape
    grid, q_spec, kv_spec, scratch, params = _attn_common(b, s, tq)
    return pl.pallas_call(
        functools.partial(_fox_attn_kernel, tk=tq),
        grid=grid,
        in_specs=[q_spec, kv_spec, kv_spec,
                  pl.BlockSpec((1, 1, tq, 2), lambda bi, u, qi: (bi, u, qi, 0)),
                  pl.BlockSpec((1, 1, s // tq, 2, tq), lambda bi, u, qi: (bi, u, 0, 0, 0))],
        out_specs=q_spec,
        out_shape=jax.ShapeDtypeStruct((b, s, FOX_WIDTH), F32),
        scratch_shapes=scratch,
        compiler_params=params,
        name="fox_attention",
    )(q, kb, vb, cq, ck)


def _diff_attention(q, kb, vb, bias, lq1, lk1, lq2, lk2, g, *, tq, lam_init):
    b, s, _ = q.shape
    grid, q_spec, kv_spec, scratch, params = _attn_common(b, s, tq)
    vec = lambda n: pl.BlockSpec((1, n), lambda bi, u, qi: (0, 0))
    return pl.pallas_call(
        functools.partial(_diff_attn_kernel, tk=tq, lam_init=lam_init),
        grid=grid,
        in_specs=[q_spec, kv_spec, kv_spec,
                  pl.BlockSpec((1, 3, tq, tq), lambda bi, u, qi: (u, 0, 0, 0)),
                  vec(HEAD_DIM), vec(HEAD_DIM), vec(HEAD_DIM), vec(HEAD_DIM), vec(LANES)],
        out_specs=q_spec,
        out_shape=jax.ShapeDtypeStruct((b, s, DIFF_WIDTH), F32),
        scratch_shapes=scratch,
        compiler_params=params,
        name="diff_attention",
    )(q, kb, vb, bias, lq1, lk1, lq2, lk2, g)


def _merge_kernel(of_ref, fz_ref, od_ref, dz_ref, ga_ref, gb_ref, x_ref, wbf_ref, wbd_ref, wo_ref, g_ref,
                  out_ref, *, final):
    fz = fz_ref[...]
    dz = dz_ref[...]
    a = (of_ref[...] * (fz * _sigmoid(fz))).astype(BF16)
    d = (od_ref[...] * (dz * _sigmoid(dz))).astype(BF16)
    y_fox = jnp.dot(a, wbf_ref[...], preferred_element_type=F32)
    y_diff = jnp.dot(d, wbd_ref[...], preferred_element_type=F32)
    m = _sigmoid(ga_ref[...]) * y_fox + _sigmoid(gb_ref[...]) * y_diff
    x = x_ref[...] + jnp.dot(m.astype(BF16), wo_ref[...], preferred_element_type=F32)
    if final:
        x = x * lax.rsqrt(jnp.mean(x * x, axis=-1, keepdims=True) + RMS_EPS) * g_ref[...]
    out_ref[...] = x


def _merge(o_f, fz, o_d, dz, ga, gb, x2d, w_bf, w_bd, w_o, g_final, *, tm, final, name):
    m, d = x2d.shape
    row_spec = lambda width: pl.BlockSpec((tm, width), lambda i: (i, 0))
    const = lambda shape: pl.BlockSpec(shape, lambda i: (0, 0), pipeline_mode=pl.Buffered(1))
    return pl.pallas_call(
        functools.partial(_merge_kernel, final=final),
        grid=(m // tm,),
        in_specs=[row_spec(512), row_spec(512), row_spec(512), row_spec(512), row_spec(d), row_spec(d),
                  row_spec(d), const(w_bf.shape), const(w_bd.shape), const(w_o.shape), const((1, d))],
        out_specs=row_spec(d),
        out_shape=jax.ShapeDtypeStruct((m, d), F32),
        compiler_params=pltpu.CompilerParams(dimension_semantics=("arbitrary",), vmem_limit_bytes=VMEM_LIMIT),
        name=name,
    )(o_f, fz, o_d, dz, ga, gb, x2d, w_bf, w_bd, w_o, g_final)


def _decode_kernel(pt_ref, *refs, n_slots, n_pages, page, lam_init):
    del pt_ref
    it = iter(refs)
    fk_refs = [next(it) for _ in range(n_slots)]
    lf_refs = [next(it) for _ in range(n_slots)]
    dk_refs = [next(it) for _ in range(n_slots)]
    fv_refs = [next(it) for _ in range(n_slots)]
    dv_refs = [next(it) for _ in range(n_slots)]
    (fq_ref, dq_ref, fkn_ref, fvn_ref, dkn_ref, dvn_ref, lfn_ref, bias_ref, bias0_ref,
     lq1_ref, lk1_ref, lq2_ref, lk2_ref, g_ref,
     of_ref, od_ref,
     af_sc, ad_sc, sf_sc, sd_sc, run_sc, selff_sc, selfd_sc, accf_sc, accd_sc) = it

    t = pl.program_id(1)
    steps = n_pages // n_slots
    width = fq_ref.shape[-1]
    hv = lax.broadcasted_iota(jnp.int32, (8, width), 0)
    lane = lax.broadcasted_iota(jnp.int32, (8, width), 1)
    own = (lane // HEAD_DIM) == hv

    @pl.when(t == 0)
    def _():
        af_sc[...] = jnp.where(own, fq_ref[0], 0.0).astype(BF16)
        ad_sc[...] = jnp.where(own, dq_ref[0], 0.0).astype(BF16)
        run_sc[...] = jnp.zeros_like(run_sc)

    @pl.when(t < steps)
    def _():
        u = lax.broadcasted_iota(jnp.int32, (page, page), 0)
        s = lax.broadcasted_iota(jnp.int32, (page, page), 1)
        tri = jnp.where(u <= s, 1.0, 0.0).astype(BF16)
        for i in range(n_slots):
            pg = t * n_slots + i
            kf = fk_refs[i][...].astype(BF16)
            qk = lax.dot_general(af_sc[...], kf, _NT, preferred_element_type=F32)
            h1, h2, h3 = _split3(lf_refs[i][...])
            pre = (jnp.dot(h1, tri, preferred_element_type=F32)
                   + jnp.dot(h2, tri, preferred_element_type=F32)
                   + jnp.dot(h3, tri, preferred_element_type=F32)) + run_sc[...]
            sf_sc[pg] = qk - pre
            run_sc[...] = jnp.broadcast_to(pre[:, page - 1:page], run_sc.shape)
            kd = dk_refs[i][...].astype(BF16)
            sd_sc[pg] = lax.dot_general(ad_sc[...], kd, _NT, preferred_element_type=F32) + bias_ref[pg]

    @pl.when(t == steps)
    def _():
        def softmax_rows(logits, self_logit):
            mx = jnp.max(jnp.max(logits, axis=0), axis=1, keepdims=True)
            mx = jnp.maximum(mx, self_logit)
            p = jnp.exp(logits - mx[None])
            p_self = jnp.exp(self_logit - mx)
            den = jnp.sum(jnp.sum(p, axis=0), axis=1, keepdims=True) + p_self
            return p / den[None], p_self / den

        c_new = run_sc[:, 0:1] + lfn_ref[0]
        kn = jnp.broadcast_to(fkn_ref[0], (8, width)).astype(BF16)
        self_f = lax.dot_general(af_sc[...], kn, _NT, preferred_element_type=F32)[:, 0:1]
        w, w_self = softmax_rows(sf_sc[...] + c_new[None], self_f + (c_new - c_new))
        sf_sc[...] = w
        selff_sc[...] = jnp.broadcast_to(w_self, selff_sc.shape)
        kn = jnp.broadcast_to(dkn_ref[0], (8, width)).astype(BF16)
        self_d = lax.dot_general(ad_sc[...], kn, _NT, preferred_element_type=F32)[:, 0:1] + bias0_ref[:, 0:1]
        w, w_self = softmax_rows(sd_sc[...], self_d)
        sd_sc[...] = w
        selfd_sc[...] = jnp.broadcast_to(w_self, selfd_sc.shape)
        accf_sc[...] = jnp.zeros_like(accf_sc)
        accd_sc[...] = jnp.zeros_like(accd_sc)

    lam = _lam_value(lq1_ref[...], lk1_ref[...], lq2_ref[...], lk2_ref[...], lam_init)
    even = (lax.broadcasted_iota(jnp.int32, (8, page), 0) % 2) == 0

    def combine(w8):
        nxt = pltpu.roll(w8, 7, 0)
        return jnp.where(even, w8 - lam * nxt, 0.0)

    @pl.when(t >= steps)
    def _():
        for i in range(n_slots):
            pg = (t - steps) * n_slots + i
            vf = fv_refs[i][...].astype(BF16)
            accf_sc[...] += jnp.dot(sf_sc[pg].astype(BF16), vf, preferred_element_type=F32)
            vd = dv_refs[i][...].astype(BF16)
            accd_sc[...] += jnp.dot(combine(sd_sc[pg]).astype(BF16), vd, preferred_element_type=F32)

    @pl.when(t == 2 * steps - 1)
    def _():
        acc = accf_sc[...] + selff_sc[:, 0:1] * fvn_ref[0]
        of_ref[0] = jnp.sum(jnp.where(own, acc, 0.0), axis=0, keepdims=True)
        wd = combine(selfd_sc[...])[:, 0:1]
        acc = accd_sc[...] + wd * dvn_ref[0]
        own_d = (lane // (2 * HEAD_DIM)) * 2 == hv
        o = jnp.sum(jnp.where(own_d, acc, 0.0), axis=0, keepdims=True)
        lane1 = lax.broadcasted_iota(jnp.int32, (1, width), 1)
        ms = jnp.zeros((1, width), F32)
        for h in range(width // LANES):
            sel = (lane1 // LANES) == h
            ms = jnp.where(sel, jnp.sum(jnp.where(sel, o * o, 0.0), axis=1, keepdims=True) / LANES, ms)
        y = o * lax.rsqrt(ms + RMS_EPS)
        od_ref[0] = (y * g_ref[...]) * (1.0 - lam_init)


def _decode_attention(page_table, base, fk, lft, dk, fv, dv, fq, dq, fkn, fvn, dkn, dvn, lfn, bias, bias0,
                      lq1, lk1, lq2, lk2, g4, *, n_slots, lam_init):
    bs, n_pages = page_table.shape
    page = fk.shape[1]
    width = fk.shape[2]
    steps = n_pages // n_slots

    def k_map(i):
        return lambda b, t, pt: (base + pt[b, jnp.minimum(t, steps - 1) * n_slots + i], 0, 0)

    def v_map(i):
        return lambda b, t, pt: (base + pt[b, jnp.maximum(t - steps, 0) * n_slots + i], 0, 0)

    page_spec = lambda fn: pl.BlockSpec((None, page, width), fn)
    row_spec = pl.BlockSpec((1, 1, width), lambda b, t, pt: (b, 0, 0))
    full = lambda a: pl.BlockSpec(a.shape, lambda b, t, pt: (0,) * a.ndim)
    in_specs = ([page_spec(k_map(i)) for i in range(n_slots)]
                + [pl.BlockSpec((None, 8, page), k_map(i)) for i in range(n_slots)]
                + [page_spec(k_map(i)) for i in range(n_slots)]
                + [page_spec(v_map(i)) for i in range(n_slots)]
                + [page_spec(v_map(i)) for i in range(n_slots)]
                + [row_spec] * 6
                + [pl.BlockSpec((1, 8, 1), lambda b, t, pt: (b, 0, 0)),
                   full(bias), full(bias0), full(lq1), full(lk1), full(lq2), full(lk2), full(g4)])
    scratch = [pltpu.VMEM((8, width), BF16), pltpu.VMEM((8, width), BF16),
               pltpu.VMEM((n_pages, 8, page), F32), pltpu.VMEM((n_pages, 8, page), F32),
               pltpu.VMEM((8, page), F32), pltpu.VMEM((8, page), F32), pltpu.VMEM((8, page), F32),
               pltpu.VMEM((8, width), F32), pltpu.VMEM((8, width), F32)]
    grid_spec = pltpu.PrefetchScalarGridSpec(
        num_scalar_prefetch=1, grid=(bs, 2 * steps), in_specs=in_specs,
        out_specs=[row_spec, row_spec], scratch_shapes=scratch)
    args = ([fk] * n_slots + [lft] * n_slots + [dk] * n_slots + [fv] * n_slots + [dv] * n_slots
            + [fq, dq, fkn, fvn, dkn, dvn, lfn, bias, bias0, lq1, lk1, lq2, lk2, g4])
    return pl.pallas_call(
        functools.partial(_decode_kernel, n_slots=n_slots, n_pages=n_pages, page=page, lam_init=lam_init),
        grid_spec=grid_spec,
        out_shape=[jax.ShapeDtypeStruct((bs, 1, width), F32), jax.ShapeDtypeStruct((bs, 1, width), F32)],
        compiler_params=pltpu.CompilerParams(dimension_semantics=("arbitrary", "arbitrary"),
                                             vmem_limit_bytes=VMEM_LIMIT),
        name="decode_attention",
    )(page_table, *args)


def kernel(x_prompt, x_sample, cache_fox_k, cache_fox_v, cache_fox_logf, cache_diff_k, cache_diff_v, page_table,
           norm_g, w_in, b_forget, lambda_q1, lambda_k1, lambda_q2, lambda_k2, diff_subln_g, w_branch_fox,
           w_branch_diff, w_out, rel_bias, final_norm_g):
    depth = w_in.shape[0]
    b, s, d = x_prompt.shape
    bs = x_sample.shape[0]
    n_pool, page = cache_fox_k.shape[1], cache_fox_k.shape[2]
    n_pages = page_table.shape[1]
    past_len = n_pages * page
    n_fox = FOX_WIDTH // HEAD_DIM
    n_diff = DIFF_WIDTH // (2 * HEAD_DIM)
    tq = 256
    tm = 256
    n_slots = 4

    w_main = jnp.concatenate([w_in[:, :, :3 * FOX_WIDTH], w_in[:, :, 3 * FOX_WIDTH + n_fox:]], axis=2).astype(BF16)
    w_f = jnp.pad(w_in[:, :, 3 * FOX_WIDTH:3 * FOX_WIDTH + n_fox], ((0, 0), (0, 0), (0, LANES - n_fox))).astype(BF16)
    b_f = jnp.pad(b_forget, ((0, 0), (0, LANES - n_fox)))
    w_bf = w_branch_fox.astype(BF16)
    w_bd = w_branch_diff.astype(BF16)
    w_o = w_out.astype(BF16)

    fk_c = cache_fox_k.reshape(depth * n_pool, page, FOX_WIDTH)
    fv_c = cache_fox_v.reshape(depth * n_pool, page, FOX_WIDTH)
    dk_c = cache_diff_k.reshape(depth * n_pool, page, DIFF_WIDTH)
    dv_c = cache_diff_v.reshape(depth * n_pool, page, DIFF_WIDTH)
    lf_c = jnp.swapaxes(cache_fox_logf, 2, 3).reshape(depth * n_pool, n_fox, page)

    bias_p = _prompt_bias(rel_bias, tq, tq)
    bias_s, bias_s0 = _decode_bias(rel_bias, past_len, page)

    xp = x_prompt.reshape(b * s, d)
    xs = x_sample.reshape(bs, d)
    outs_p = [[] for _ in range(5)]
    outs_s = [[] for _ in range(5)]
    for l in range(depth):
        lam_init = _lambda_init(l)
        final = l == depth - 1
        g = norm_g[l][None]
        lvec = [a[l][None] for a in (lambda_q1, lambda_k1, lambda_q2, lambda_k2)]
        g_sub = diff_subln_g[l][None]

        (fq, fk, fkb, fv, fvb, fz, dq, dk, dkb, dv, dvb, dz, ga, gb, lf, c, ct) = _inproj(
            xp, g, w_main[l], w_f[l], b_f[l][None], tm=tm, rows_per_seq=s, with_cumsum=True)
        to3 = lambda a: a.reshape(b, s, -1)
        cq = c.reshape(b, s, n_fox // 2, 2).transpose(0, 2, 1, 3)
        ck = ct.reshape(n_fox // 2, 2, b, s // tq, tq).transpose(2, 0, 3, 1, 4)
        o_f = _fox_attention(to3(fq), to3(fkb), to3(fvb), cq, ck, tq=tq)
        o_d = _diff_attention(to3(dq), to3(dkb), to3(dvb), bias_p, *lvec, g_sub, tq=tq, lam_init=lam_init)
        xp = _merge(o_f.reshape(b * s, -1), fz, o_d.reshape(b * s, -1), dz, ga, gb, xp, w_bf[l], w_bd[l], w_o[l],
                    final_norm_g[None], tm=tm, final=final, name="merge_prompt")
        for acc, a in zip(outs_p, (fk, fv, lf, dk, dv)):
            acc.append(a)

        (sq, sk, _, sv, _, sz, sdq, sdk, _, sdv, _, sdz, sga, sgb, slf) = _inproj(
            xs, g, w_main[l], w_f[l], b_f[l][None], tm=bs, rows_per_seq=1, with_cumsum=False)
        row3 = lambda a: a.reshape(bs, 1, -1)
        o_fs, o_ds = _decode_attention(
            page_table, l * n_pool, fk_c, lf_c, dk_c, fv_c, dv_c,
            row3(sq.astype(F32)), row3(sdq.astype(F32)), row3(sk), row3(sv), row3(sdk), row3(sdv), slf.reshape(bs, n_fox, 1),
            bias_s, bias_s0, *lvec, jnp.tile(g_sub, (1, n_diff)), n_slots=n_slots, lam_init=lam_init)
        xs = _merge(o_fs.reshape(bs, -1), sz, o_ds.reshape(bs, -1), sdz, sga, sgb, xs, w_bf[l], w_bd[l], w_o[l],
                    final_norm_g[None], tm=bs, final=final, name="merge_sample")
        for acc, a in zip(outs_s, (sk, sv, slf, sdk, sdv)):
            acc.append(a)

    y_prompt = xp.reshape(b, s, d)
    y_sample = xs.reshape(bs, 1, d)
    pk, pv, plf, pdk, pdv = [jnp.stack(a) for a in outs_p]
    sk_l, sv_l, slf_l, sdk_l, sdv_l = [jnp.stack(a) for a in outs_s]
    return (y_prompt, y_sample,
            pk.reshape(depth, b, s, n_fox, HEAD_DIM), pv.reshape(depth, b, s, n_fox, HEAD_DIM),
            plf.reshape(depth, b, s, n_fox),
            pdk.reshape(depth, b, s, n_diff, 2, HEAD_DIM), pdv.reshape(depth, b, s, n_diff, 2 * HEAD_DIM),
            sk_l.reshape(depth, bs, 1, n_fox, HEAD_DIM), sv_l.reshape(depth, bs, 1, n_fox, HEAD_DIM),
            slf_l.reshape(depth, bs, 1, n_fox),
            sdk_l.reshape(depth, bs, 1, n_diff, 2, HEAD_DIM), sdv_l.reshape(depth, bs, 1, n_diff, 2 * HEAD_DIM))
```

```python
import functools
import math

import jax
import jax.numpy as jnp
from jax import lax
from jax.experimental import pallas as pl
from jax.experimental.pallas import tpu as pltpu

HEAD_DIM = 64
LANES = 128
FOX_WIDTH = 512
DIFF_WIDTH = 512
NUM_BUCKETS = 32
MAX_DISTANCE = 128
RMS_EPS = 1e-6
NEG_INF = -1e30
SCALE = HEAD_DIM ** -0.5
LOG2E = math.log2(math.e)
SUM_ROWS = 16
VMEM_LIMIT = 56 * 1024 * 1024

F32 = jnp.float32
BF16 = jnp.bfloat16

_NT = (((1,), (1,)), ((), ()))


def _lambda_init(layer):
    return 0.8 - 0.6 * math.exp(-0.3 * layer)


def _sigmoid(x):
    return 1.0 / (1.0 + jnp.exp(-x))


def _log_sigmoid(x):
    return jnp.minimum(x, 0.0) - jnp.log1p(jnp.exp(-jnp.abs(x)))


def _split3(x):
    h1 = x.astype(BF16)
    r1 = x - h1.astype(F32)
    h2 = r1.astype(BF16)
    r2 = r1 - h2.astype(F32)
    return h1, h2, r2.astype(BF16)


def _prefix_sums(x, n):
    u = lax.broadcasted_iota(jnp.int32, (n, n), 0)
    s = lax.broadcasted_iota(jnp.int32, (n, n), 1)
    tri = jnp.where(u <= s, 1.0, 0.0).astype(BF16)
    h1, h2, h3 = _split3(x)
    return (jnp.dot(h1, tri, preferred_element_type=F32) + jnp.dot(h2, tri, preferred_element_type=F32)
            + jnp.dot(h3, tri, preferred_element_type=F32))


def _t5_bucket(rel):
    n = jnp.maximum(rel, 0)
    max_exact = NUM_BUCKETS // 2
    nf = jnp.maximum(n, 1).astype(F32)
    large = max_exact + (jnp.log(nf / max_exact) / math.log(MAX_DISTANCE / max_exact)
                         * (NUM_BUCKETS - max_exact)).astype(jnp.int32)
    large = jnp.minimum(large, NUM_BUCKETS - 1)
    return jnp.where(n < max_exact, n, large)


def _lam_value(lq1, lk1, lq2, lk2, lam_init):
    a = jnp.sum(lq1 * lk1, axis=1, keepdims=True)
    b = jnp.sum(lq2 * lk2, axis=1, keepdims=True)
    return jnp.exp(a) - jnp.exp(b) + lam_init


def _rms_rows(x, g_ref):
    y = x * lax.rsqrt(jnp.mean(x * x, axis=-1, keepdims=True) + RMS_EPS)
    return y * g_ref[...]


_FQ, _FK, _FV, _FZ, _DQ, _DK, _DV, _DZ, _GA, _GB = (0, 512, 1024, 1536, 2048, 2560, 3072, 3584, 4096, 5120)


def _inproj_prompt_kernel(*refs, n_alias):
    x_ref, g_ref, w_ref, wf_ref, bf_ref = refs[:5]
    (fqt_ref, fk_ref, fvt_ref, fz_ref, dqt_ref, dk_ref, dvt_ref, dz_ref, ga_ref, gb_ref, ct_ref, c_ref,
     pk_ref, pv_ref, plf_ref, pdk_ref, pdv_ref, carry_ref) = refs[5 + n_alias:]
    hb = _rms_rows(x_ref[...], g_ref).astype(BF16)
    tm = hb.shape[0]

    def tr(r0, n):
        return lax.dot_general(w_ref[r0:r0 + n, :], hb, _NT, preferred_element_type=F32)

    def nat(r0, n):
        return lax.dot_general(hb, w_ref[r0:r0 + n, :], _NT, preferred_element_type=F32)

    fqt_ref[0] = (tr(_FQ, 512) * (SCALE * LOG2E)).astype(BF16)
    kt = tr(_FK, 512)
    pk_ref[...] = kt
    fk_ref[...] = kt.T.astype(BF16)
    vt = tr(_FV, 512)
    pv_ref[...] = vt
    fvt_ref[0] = vt.astype(BF16)
    fz_ref[...] = nat(_FZ, 512)
    dqt_ref[0] = (tr(_DQ, 512) * (SCALE * LOG2E)).astype(BF16)
    kt = tr(_DK, 512)
    pdk_ref[...] = kt
    dk_ref[...] = kt.T.astype(BF16)
    v = nat(_DV, 512)
    pdv_ref[...] = v
    dvt_ref[0] = v.T.astype(BF16)
    dz_ref[...] = nat(_DZ, 512)
    ga_ref[...] = nat(_GA, 1024)
    gb_ref[...] = nat(_GB, 1024)

    fft = lax.dot_general(wf_ref[...], hb, _NT, preferred_element_type=F32) + bf_ref[...]
    lft = _log_sigmoid(fft)
    plf_ref[...] = lft[:8]

    @pl.when(pl.program_id(1) == 0)
    def _():
        carry_ref[...] = jnp.zeros_like(carry_ref)

    ct = _prefix_sums(lft, tm) + carry_ref[...]
    carry_ref[...] = ct[:, tm - 1:tm]
    ct_ref[...] = ct[:8]
    c_ref[...] = ct.T[:, :8]


def _inproj_prompt(x3d, g, w_t, wf_t, bf_col, stacked, layer, depth, *, tm):
    b, s, d = x3d.shape
    tps = s // tm
    n_alias = len(stacked)
    im2 = lambda bi, i: (bi * tps + i, 0)
    im3 = lambda bi, i: (bi * tps + i, 0, 0)
    const = lambda shape: pl.BlockSpec(shape, lambda bi, i: (0,) * len(shape), pipeline_mode=pl.Buffered(1))
    m = b * s
    out_shape = [
        jax.ShapeDtypeStruct((m // tm, 512, tm), BF16),
        jax.ShapeDtypeStruct((m, 512), BF16),
        jax.ShapeDtypeStruct((m // tm, 512, tm), BF16),
        jax.ShapeDtypeStruct((m, 512), F32),
        jax.ShapeDtypeStruct((m // tm, 512, tm), BF16),
        jax.ShapeDtypeStruct((m, 512), BF16),
        jax.ShapeDtypeStruct((m // tm, 512, tm), BF16),
        jax.ShapeDtypeStruct((m, 512), F32),
        jax.ShapeDtypeStruct((m, 1024), F32),
        jax.ShapeDtypeStruct((m, 1024), F32),
        jax.ShapeDtypeStruct((b, 8, s), F32),
        jax.ShapeDtypeStruct((m, 8), F32),
        jax.ShapeDtypeStruct((depth, b, 512, s), F32),
        jax.ShapeDtypeStruct((depth, b, 512, s), F32),
        jax.ShapeDtypeStruct((depth, b, 8, s), F32),
        jax.ShapeDtypeStruct((depth, b, 512, s), F32),
        jax.ShapeDtypeStruct((depth, b, s, 512), F32),
    ]
    t_spec = pl.BlockSpec((1, 512, tm), im3)
    r_spec = lambda width: pl.BlockSpec((tm, width), im2)
    st_spec = lambda rows: pl.BlockSpec((None, None, rows, tm), lambda bi, i: (layer, bi, 0, i))
    out_specs = [t_spec, r_spec(512), t_spec, r_spec(512), t_spec, r_spec(512), t_spec, r_spec(512),
                 r_spec(1024), r_spec(1024),
                 pl.BlockSpec((None, 8, tm), lambda bi, i: (bi, 0, i)), r_spec(8),
                 st_spec(512), st_spec(512), st_spec(8), st_spec(512),
                 pl.BlockSpec((None, None, tm, 512), lambda bi, i: (layer, bi, i, 0))]
    in_specs = ([pl.BlockSpec((None, tm, d), lambda bi, i: (bi, i, 0)), const((1, d)), const(w_t.shape),
                 const(wf_t.shape), const(bf_col.shape)]
                + [pl.BlockSpec(memory_space=pl.ANY)] * n_alias)
    return pl.pallas_call(
        functools.partial(_inproj_prompt_kernel, n_alias=n_alias),
        grid=(b, tps),
        in_specs=in_specs,
        out_specs=out_specs,
        out_shape=out_shape,
        scratch_shapes=[pltpu.VMEM((LANES, 1), F32)],
        input_output_aliases={5 + k: 12 + k for k in range(n_alias)},
        compiler_params=pltpu.CompilerParams(dimension_semantics=("arbitrary", "arbitrary"),
                                             vmem_limit_bytes=VMEM_LIMIT),
        name="inproj_prompt",
    )(x3d, g, w_t, wf_t, bf_col, *stacked)


def _inproj_sample_kernel(x_ref, g_ref, w_ref, wf_ref, bf_ref,
                          fq_ref, fk_ref, fv_ref, fz_ref, dq_ref, dk_ref, dv_ref, dz_ref, ga_ref, gb_ref, lf_ref):
    hb = _rms_rows(x_ref[...], g_ref).astype(BF16)

    def nat(r0, n):
        return lax.dot_general(hb, w_ref[r0:r0 + n, :], _NT, preferred_element_type=F32)

    fq_ref[...] = nat(_FQ, 512) * SCALE
    fk_ref[...] = nat(_FK, 512)
    fv_ref[...] = nat(_FV, 512)
    fz_ref[...] = nat(_FZ, 512)
    dq_ref[...] = nat(_DQ, 512) * SCALE
    dk_ref[...] = nat(_DK, 512)
    dv_ref[...] = nat(_DV, 512)
    dz_ref[...] = nat(_DZ, 512)
    ga_ref[...] = nat(_GA, 1024)
    gb_ref[...] = nat(_GB, 1024)
    ff = lax.dot_general(hb, wf_ref[...], _NT, preferred_element_type=F32) + bf_ref[...]
    lf_ref[...] = _log_sigmoid(ff)[:, :8]


def _inproj_sample(x2d, g, w_t, wf_t, bf_row):
    m, d = x2d.shape
    full = lambda shape: pl.BlockSpec(shape, lambda i: (0,) * len(shape))
    widths = [512] * 8 + [1024, 1024, 8]
    return pl.pallas_call(
        _inproj_sample_kernel,
        grid=(1,),
        in_specs=[full((m, d)), full((1, d)), full(w_t.shape), full(wf_t.shape), full(bf_row.shape)],
        out_specs=[full((m, w)) for w in widths],
        out_shape=[jax.ShapeDtypeStruct((m, w), F32) for w in widths],
        compiler_params=pltpu.CompilerParams(dimension_semantics=("arbitrary",), vmem_limit_bytes=VMEM_LIMIT),
        name="inproj_sample",
    )(x2d, g, w_t, wf_t, bf_row)


def _bias_lookup(rb_ref, bucket, head):
    out = jnp.zeros(bucket.shape, F32)
    for b in range(NUM_BUCKETS):
        out = jnp.where(bucket == b, rb_ref[b, head], out)
    return out


def _prompt_bias_kernel(rb_ref, out_ref, *, tq, tk):
    h = pl.program_id(0)
    key = lax.broadcasted_iota(jnp.int32, (tk, tq), 0)
    qry = lax.broadcasted_iota(jnp.int32, (tk, tq), 1)
    for t in range(3):
        bucket = _t5_bucket(qry - key + t * tk)
        out_ref[0, t] = _bias_lookup(rb_ref, bucket, h) * LOG2E


def _prompt_bias(rel_bias, tq, tk):
    return pl.pallas_call(
        functools.partial(_prompt_bias_kernel, tq=tq, tk=tk),
        grid=(4,),
        in_specs=[pl.BlockSpec(memory_space=pltpu.SMEM)],
        out_specs=pl.BlockSpec((1, 3, tk, tq), lambda h: (h, 0, 0, 0)),
        out_shape=jax.ShapeDtypeStruct((4, 3, tk, tq), F32),
        name="prompt_bias",
    )(rel_bias)


def _decode_bias_kernel(rb_ref, out_ref, self_ref, *, past_len, page):
    p = pl.program_id(0)
    pos = p * page + lax.broadcasted_iota(jnp.int32, (8, page), 1)
    row = lax.broadcasted_iota(jnp.int32, (8, page), 0)
    bucket = _t5_bucket(past_len - pos)
    bucket0 = _t5_bucket(jnp.zeros((8, page), jnp.int32))
    acc = jnp.zeros((8, page), F32)
    acc0 = jnp.zeros((8, page), F32)
    for hh in range(4):
        sel = (row // 2) == hh
        acc = jnp.where(sel, _bias_lookup(rb_ref, bucket, hh), acc)
        acc0 = jnp.where(sel, _bias_lookup(rb_ref, bucket0, hh), acc0)
    out_ref[0] = acc
    self_ref[...] = acc0


def _decode_bias(rel_bias, past_len, page):
    n_pages = past_len // page
    return pl.pallas_call(
        functools.partial(_decode_bias_kernel, past_len=past_len, page=page),
        grid=(n_pages,),
        in_specs=[pl.BlockSpec(memory_space=pltpu.SMEM)],
        out_specs=[pl.BlockSpec((1, 8, page), lambda p: (p, 0, 0)), pl.BlockSpec((8, page), lambda p: (0, 0))],
        out_shape=[jax.ShapeDtypeStruct((n_pages, 8, page), F32), jax.ShapeDtypeStruct((8, page), F32)],
        name="decode_bias",
    )(rel_bias)


def _attn_chunk(j, masked, qs, k_ref, vt_ref, sub_term, add_term, col_terms, m_sc, acc_sc, tk):
    start = pl.multiple_of(j * tk, tk)
    tq = qs[0].shape[1]
    if masked:
        key = lax.broadcasted_iota(jnp.int32, (tk, tq), 0)
        qry = lax.broadcasted_iota(jnp.int32, (tk, tq), 1)
        keep = key <= qry
    ts = []
    for n, qn in enumerate(qs):
        u = n // 2
        ks = k_ref[0, pl.ds(start, tk), u * LANES:(u + 1) * LANES]
        t = jnp.dot(ks, qn, preferred_element_type=F32)
        if sub_term is not None:
            t = t - sub_term(n, start)
        if add_term is not None:
            t = t + add_term(n)
        if masked:
            t = jnp.where(keep, t, NEG_INF)
        ts.append(t)
    ps, alphas = [], []
    for n, t in enumerate(ts):
        ct = col_terms[n]
        mcol = jnp.max(t, axis=0, keepdims=True)
        if ct is not None:
            mcol = mcol + ct
        m_old = m_sc[n]
        m_new = jnp.maximum(m_old, mcol)
        shift = m_new if ct is None else m_new - ct
        ps.append(jnp.exp2(t - shift).astype(BF16))
        alphas.append(jnp.exp2(m_old - m_new))
        m_sc[n] = m_new
    ones = jnp.ones((SUM_ROWS, tk), BF16)
    for n, p in enumerate(ps):
        u = n // 2
        vts = jnp.concatenate([vt_ref[0, j, u * LANES:(u + 1) * LANES, :], ones], axis=0)
        acc_sc[n] = alphas[n] * acc_sc[n] + jnp.dot(vts, p, preferred_element_type=F32)


def _attn_init(m_sc, acc_sc):
    m_sc[...] = jnp.full(m_sc.shape, -jnp.inf, F32)
    acc_sc[...] = jnp.zeros_like(acc_sc)


def _attn_result(acc_sc, n):
    return acc_sc[n, :LANES, :] / acc_sc[n, LANES:LANES + 1, :]


def _split_q(qt_ref, n_units):
    qs = []
    for u in range(n_units):
        qt = qt_ref[0, 0, u * LANES:(u + 1) * LANES, :]
        row = lax.broadcasted_iota(jnp.int32, qt.shape, 0)
        zero = jnp.zeros_like(qt)
        qs += [jnp.where(row < HEAD_DIM, qt, zero), jnp.where(row >= HEAD_DIM, qt, zero)]
    return qs


def _fox_attn_kernel(qt_ref, k_ref, vt_ref, cq_ref, ck_ref, o_ref, ckb_sc, m_sc, acc_sc, *, tk, n_units):
    qi = pl.program_id(2)
    n_streams = 2 * n_units

    @pl.when(qi == 0)
    def _():
        ck = ck_ref[0, 0] * LOG2E
        for n in range(n_streams):
            ckb_sc[n] = jnp.broadcast_to(ck[:, n:n + 1], ckb_sc.shape[1:])

    qs = _split_q(qt_ref, n_units)
    cq = cq_ref[0, 0] * LOG2E
    col_terms = [cq[n:n + 1, :] for n in range(n_streams)]
    _attn_init(m_sc, acc_sc)

    def key_decay(n, start):
        ckn = ckb_sc[n, pl.ds(start, tk), :]
        return jnp.concatenate([ckn] * (qs[0].shape[1] // LANES), axis=1)

    def step(j, masked):
        _attn_chunk(j, masked, qs, k_ref, vt_ref, key_decay, None, col_terms, m_sc, acc_sc, tk)

    def body(j, carry):
        step(j, False)
        return carry

    lax.fori_loop(0, qi, body, 0)
    step(qi, True)
    row = lax.broadcasted_iota(jnp.int32, (LANES, qs[0].shape[1]), 0)
    for u in range(n_units):
        ot = jnp.where(row < HEAD_DIM, _attn_result(acc_sc, 2 * u), _attn_result(acc_sc, 2 * u + 1))
        o_ref[0, :, u * LANES:(u + 1) * LANES] = ot.T


def _diff_attn_kernel(qt_ref, k_ref, vt_ref, bias_ref, lq1_ref, lk1_ref, lq2_ref, lk2_ref, g_ref,
                      o_ref, m_sc, acc_sc, *, tk, n_units, lam_init):
    qi = pl.program_id(2)
    n_streams = 2 * n_units
    qs = _split_q(qt_ref, n_units)
    _attn_init(m_sc, acc_sc)

    def step(j, tile, masked):
        _attn_chunk(j, masked, qs, k_ref, vt_ref, None, lambda n: bias_ref[n // 2, tile], [None] * n_streams,
                    m_sc, acc_sc, tk)

    def body(j, carry):
        step(j, 2, False)
        return carry

    lax.fori_loop(0, qi - 1, body, 0)

    @pl.when(qi >= 1)
    def _():
        step(qi - 1, 1, False)

    step(qi, 0, True)
    lam = _lam_value(lq1_ref[...], lk1_ref[...], lq2_ref[...], lk2_ref[...], lam_init)
    for u in range(n_units):
        ot = _attn_result(acc_sc, 2 * u) - lam * _attn_result(acc_sc, 2 * u + 1)
        y = ot * lax.rsqrt(jnp.mean(ot * ot, axis=0, keepdims=True) + RMS_EPS)
        o_ref[0, :, u * LANES:(u + 1) * LANES] = ((y * g_ref[...]) * (1.0 - lam_init)).T


def _attn_common(b, s, tq, n_units):
    width = n_units * LANES
    nq = s // tq
    grid = (b, FOX_WIDTH // width, nq)
    qt_spec = pl.BlockSpec((1, 1, width, tq), lambda bi, g, qi: (bi, qi, g, 0))
    k_spec = pl.BlockSpec((1, s, width), lambda bi, g, qi: (bi, 0, g))
    vt_spec = pl.BlockSpec((1, nq, width, tq), lambda bi, g, qi: (bi, 0, g, 0))
    o_spec = pl.BlockSpec((1, tq, width), lambda bi, g, qi: (bi, qi, g))
    n_streams = 2 * n_units
    scratch = [pltpu.VMEM((n_streams, 1, tq), F32), pltpu.VMEM((n_streams, LANES + SUM_ROWS, tq), F32)]
    params = pltpu.CompilerParams(dimension_semantics=("arbitrary", "arbitrary", "arbitrary"),
                                  vmem_limit_bytes=VMEM_LIMIT)
    return grid, qt_spec, k_spec, vt_spec, o_spec, scratch, params


def _fox_attention(qt, kb, vt, cq, ck, *, tq, n_units):
    b, s, _ = kb.shape
    grid, qt_spec, k_spec, vt_spec, o_spec, scratch, params = _attn_common(b, s, tq, n_units)
    n_streams = 2 * n_units
    return pl.pallas_call(
        functools.partial(_fox_attn_kernel, tk=tq, n_units=n_units),
        grid=grid,
        in_specs=[qt_spec, k_spec, vt_spec,
                  pl.BlockSpec((1, 1, n_streams, tq), lambda bi, g, qi: (bi, g, 0, qi)),
                  pl.BlockSpec((1, 1, s, n_streams), lambda bi, g, qi: (bi, g, 0, 0))],
        out_specs=o_spec,
        out_shape=jax.ShapeDtypeStruct((b, s, FOX_WIDTH), F32),
        scratch_shapes=[pltpu.VMEM((n_streams, s, LANES), F32)] + scratch,
        compiler_params=params,
        name="fox_attention",
    )(qt, kb, vt, cq, ck)


def _diff_attention(qt, kb, vt, bias, lq1, lk1, lq2, lk2, g_col, *, tq, n_units, lam_init):
    b, s, _ = kb.shape
    grid, qt_spec, k_spec, vt_spec, o_spec, scratch, params = _attn_common(b, s, tq, n_units)
    full = lambda a: pl.BlockSpec(a.shape, lambda bi, g, qi: (0,) * a.ndim)
    return pl.pallas_call(
        functools.partial(_diff_attn_kernel, tk=tq, n_units=n_units, lam_init=lam_init),
        grid=grid,
        in_specs=[qt_spec, k_spec, vt_spec,
                  pl.BlockSpec((n_units, 3, tq, tq), lambda bi, g, qi: (g, 0, 0, 0)),
                  full(lq1), full(lk1), full(lq2), full(lk2), full(g_col)],
        out_specs=o_spec,
        out_shape=jax.ShapeDtypeStruct((b, s, DIFF_WIDTH), F32),
        scratch_shapes=scratch,
        compiler_params=params,
        name="diff_attention",
    )(qt, kb, vt, bias, lq1, lk1, lq2, lk2, g_col)


def _merge_kernel(of_ref, fz_ref, od_ref, dz_ref, ga_ref, gb_ref, x_ref, wbf_ref, wbd_ref, wo_ref, g_ref,
                  out_ref, *, final):
    fz = fz_ref[...]
    dz = dz_ref[...]
    a = (of_ref[...] * (fz * _sigmoid(fz))).astype(BF16)
    d = (od_ref[...] * (dz * _sigmoid(dz))).astype(BF16)
    y_fox = jnp.dot(a, wbf_ref[...], preferred_element_type=F32)
    y_diff = jnp.dot(d, wbd_ref[...], preferred_element_type=F32)
    m = _sigmoid(ga_ref[...]) * y_fox + _sigmoid(gb_ref[...]) * y_diff
    x = x_ref[...] + jnp.dot(m.astype(BF16), wo_ref[...], preferred_element_type=F32)
    if final:
        x = _rms_rows(x, g_ref)
    out_ref[...] = x


def _merge(o_f, fz, o_d, dz, ga, gb, x2d, w_bf, w_bd, w_o, g_final, *, tm, final, name):
    m, d = x2d.shape
    row_spec = lambda width: pl.BlockSpec((tm, width), lambda i: (i, 0))
    const = lambda shape: pl.BlockSpec(shape, lambda i: (0, 0), pipeline_mode=pl.Buffered(1))
    return pl.pallas_call(
        functools.partial(_merge_kernel, final=final),
        grid=(m // tm,),
        in_specs=[row_spec(512), row_spec(512), row_spec(512), row_spec(512), row_spec(d), row_spec(d),
                  row_spec(d), const(w_bf.shape), const(w_bd.shape), const(w_o.shape), const((1, d))],
        out_specs=row_spec(d),
        out_shape=jax.ShapeDtypeStruct((m, d), F32),
        compiler_params=pltpu.CompilerParams(dimension_semantics=("arbitrary",), vmem_limit_bytes=VMEM_LIMIT),
        name=name,
    )(o_f, fz, o_d, dz, ga, gb, x2d, w_bf, w_bd, w_o, g_final)


def _decode_kernel(pt_ref, *refs, n_slots, n_pages, page, lam_init):
    del pt_ref
    it = iter(refs)
    fk_refs = [next(it) for _ in range(n_slots)]
    lf_refs = [next(it) for _ in range(n_slots)]
    dk_refs = [next(it) for _ in range(n_slots)]
    fv_refs = [next(it) for _ in range(n_slots)]
    dv_refs = [next(it) for _ in range(n_slots)]
    (fq_ref, dq_ref, fkn_ref, fvn_ref, dkn_ref, dvn_ref, lfn_ref, bias_ref, bias0_ref,
     lq1_ref, lk1_ref, lq2_ref, lk2_ref, g_ref,
     of_ref, od_ref,
     af_sc, ad_sc, rep_sc, sf_sc, sd_sc, run_sc, selff_sc, selfd_sc, accf_sc, accd_sc) = it

    t = pl.program_id(1)
    steps = n_pages // n_slots
    width = fq_ref.shape[-1]
    n_diff = width // LANES
    hv = lax.broadcasted_iota(jnp.int32, (8, width), 0)
    lane = lax.broadcasted_iota(jnp.int32, (8, width), 1)
    own = (lane // HEAD_DIM) == hv

    @pl.when(t == 0)
    def _():
        af_sc[...] = jnp.where(own, fq_ref[0], 0.0).astype(BF16)
        ad_sc[...] = jnp.where(own, dq_ref[0], 0.0).astype(BF16)
        run_sc[...] = jnp.zeros_like(run_sc)
        pos = lax.broadcasted_iota(jnp.int32, (page, page * n_diff), 0)
        col = lax.broadcasted_iota(jnp.int32, (page, page * n_diff), 1)
        rep_sc[...] = jnp.where(col // n_diff == pos, 1.0, 0.0).astype(BF16)

    @pl.when(t < steps)
    def _():
        for i in range(n_slots):
            pg = t * n_slots + i
            qk = jnp.dot(af_sc[...], fk_refs[i][...].astype(BF16), preferred_element_type=F32)
            pre = _prefix_sums(lf_refs[i][...], page) + run_sc[...]
            sf_sc[pg] = qk - pre
            run_sc[...] = jnp.broadcast_to(pre[:, page - 1:page], run_sc.shape)
            sd_sc[pg] = (jnp.dot(ad_sc[...], dk_refs[i][...].astype(BF16), preferred_element_type=F32)
                         + bias_ref[pg])

    @pl.when(t == steps)
    def _():
        def softmax_rows(logits, self_logit):
            mx = jnp.max(jnp.max(logits, axis=0), axis=1, keepdims=True)
            mx = jnp.maximum(mx, self_logit)
            p = jnp.exp(logits - mx[None])
            p_self = jnp.exp(self_logit - mx)
            den = jnp.sum(jnp.sum(p, axis=0), axis=1, keepdims=True) + p_self
            return p / den[None], p_self / den

        def self_logit(a_sc, kn_ref):
            kn = kn_ref[0].astype(BF16).astype(F32)
            return jnp.sum(a_sc[...].astype(F32) * kn, axis=1, keepdims=True)

        c_new = run_sc[:, 0:1] + lfn_ref[0]
        w, w_self = softmax_rows(sf_sc[...] + c_new[None], self_logit(af_sc, fkn_ref) + (c_new - c_new))
        sf_sc[...] = w
        selff_sc[...] = jnp.broadcast_to(w_self, selff_sc.shape)
        w, w_self = softmax_rows(sd_sc[...], self_logit(ad_sc, dkn_ref) + bias0_ref[:, 0:1])
        sd_sc[...] = w
        selfd_sc[...] = jnp.broadcast_to(w_self, selfd_sc.shape)
        accf_sc[...] = jnp.zeros_like(accf_sc)
        accd_sc[...] = jnp.zeros_like(accd_sc)

    lam = _lam_value(lq1_ref[...], lk1_ref[...], lq2_ref[...], lk2_ref[...], lam_init)
    row8 = lax.broadcasted_iota(jnp.int32, (8, page), 0)

    def combine(w8):
        nxt = pltpu.roll(w8, 7, 0)
        return jnp.where(row8 % 2 == 0, w8 - lam * nxt, 0.0)

    @pl.when(t >= steps)
    def _():
        rowx = lax.broadcasted_iota(jnp.int32, (8, page * n_diff), 0)
        colx = lax.broadcasted_iota(jnp.int32, (8, page * n_diff), 1)
        head_cols = (colx % n_diff) * 2 == rowx
        for i in range(n_slots):
            pg = (t - steps) * n_slots + i
            accf_sc[...] += lax.dot_general(sf_sc[pg].astype(BF16), fv_refs[i][...].astype(BF16), _NT,
                                            preferred_element_type=F32)
            spread = jnp.dot(combine(sd_sc[pg]).astype(BF16), rep_sc[...], preferred_element_type=F32)
            w4 = jnp.where(head_cols, spread, 0.0).astype(BF16)
            accd_sc[...] += jnp.dot(w4, dv_refs[i][...].astype(BF16), preferred_element_type=F32)

    @pl.when(t == 2 * steps - 1)
    def _():
        acc = accf_sc[...] + selff_sc[:, 0:1] * fvn_ref[0]
        of_ref[0] = jnp.sum(jnp.where(own, acc, 0.0), axis=0, keepdims=True)
        dvn = dvn_ref[0]
        zero = jnp.zeros((1, LANES), F32)
        vn8 = jnp.concatenate([piece for h in range(n_diff)
                               for piece in (dvn[:, h * LANES:(h + 1) * LANES], zero)], axis=0)
        od8 = accd_sc[...] + combine(selfd_sc[...])[:, 0:1] * vn8
        y8 = od8 * lax.rsqrt(jnp.mean(od8 * od8, axis=1, keepdims=True) + RMS_EPS)
        y8 = (y8 * g_ref[...]) * (1.0 - lam_init)
        od_ref[0] = jnp.concatenate([y8[2 * h:2 * h + 1] for h in range(n_diff)], axis=1)


def _decode_attention(page_table, base, fk, lft, dk, fv, dv, fq, dq, fkn, fvn, dkn, dvn, lfn, bias, bias0,
                      lq1, lk1, lq2, lk2, g, *, n_slots, lam_init):
    bs, n_pages = page_table.shape
    width, page = fk.shape[1], fk.shape[2]
    steps = n_pages // n_slots

    def k_map(i):
        return lambda b, t, pt: (base + pt[b, jnp.minimum(t, steps - 1) * n_slots + i], 0, 0)

    def v_map(i):
        return lambda b, t, pt: (base + pt[b, jnp.maximum(t - steps, 0) * n_slots + i], 0, 0)

    row_spec = pl.BlockSpec((1, 1, width), lambda b, t, pt: (b, 0, 0))
    full = lambda a: pl.BlockSpec(a.shape, lambda b, t, pt: (0,) * a.ndim)
    in_specs = ([pl.BlockSpec((None, width, page), k_map(i)) for i in range(n_slots)]
                + [pl.BlockSpec((None, 8, page), k_map(i)) for i in range(n_slots)]
                + [pl.BlockSpec((None, width, page), k_map(i)) for i in range(n_slots)]
                + [pl.BlockSpec((None, width, page), v_map(i)) for i in range(n_slots)]
                + [pl.BlockSpec((None,) + dv.shape[1:], v_map(i)) for i in range(n_slots)]
                + [row_spec] * 6
                + [pl.BlockSpec((1, 8, 1), lambda b, t, pt: (b, 0, 0)),
                   full(bias), full(bias0), full(lq1), full(lk1), full(lq2), full(lk2), full(g)])
    scratch = [pltpu.VMEM((8, width), BF16), pltpu.VMEM((8, width), BF16),
               pltpu.VMEM((page, dv.shape[1]), BF16),
               pltpu.VMEM((n_pages, 8, page), F32), pltpu.VMEM((n_pages, 8, page), F32),
               pltpu.VMEM((8, page), F32), pltpu.VMEM((8, page), F32), pltpu.VMEM((8, page), F32),
               pltpu.VMEM((8, width), F32), pltpu.VMEM((8, LANES), F32)]
    grid_spec = pltpu.PrefetchScalarGridSpec(
        num_scalar_prefetch=1, grid=(bs, 2 * steps), in_specs=in_specs,
        out_specs=[row_spec, row_spec], scratch_shapes=scratch)
    args = ([fk] * n_slots + [lft] * n_slots + [dk] * n_slots + [fv] * n_slots + [dv] * n_slots
            + [fq, dq, fkn, fvn, dkn, dvn, lfn, bias, bias0, lq1, lk1, lq2, lk2, g])
    return pl.pallas_call(
        functools.partial(_decode_kernel, n_slots=n_slots, n_pages=n_pages, page=page, lam_init=lam_init),
        grid_spec=grid_spec,
        out_shape=[jax.ShapeDtypeStruct((bs, 1, width), F32), jax.ShapeDtypeStruct((bs, 1, width), F32)],
        compiler_params=pltpu.CompilerParams(dimension_semantics=("arbitrary", "arbitrary"),
                                             vmem_limit_bytes=VMEM_LIMIT),
        name="decode_attention",
    )(page_table, *args)


def kernel(x_prompt, x_sample, cache_fox_k, cache_fox_v, cache_fox_logf, cache_diff_k, cache_diff_v, page_table,
           norm_g, w_in, b_forget, lambda_q1, lambda_k1, lambda_q2, lambda_k2, diff_subln_g, w_branch_fox,
           w_branch_diff, w_out, rel_bias, final_norm_g):
    depth = w_in.shape[0]
    b, s, d = x_prompt.shape
    bs = x_sample.shape[0]
    n_pool, page = cache_fox_k.shape[1], cache_fox_k.shape[2]
    n_pages = page_table.shape[1]
    past_len = n_pages * page
    n_fox = FOX_WIDTH // HEAD_DIM
    n_diff = DIFF_WIDTH // (2 * HEAD_DIM)
    tq = 256
    n_units = 4
    n_slots = 4

    w_t = jnp.swapaxes(w_in, 1, 2)
    w_main = jnp.concatenate([w_t[:, :3 * FOX_WIDTH], w_t[:, 3 * FOX_WIDTH + n_fox:]], axis=1).astype(BF16)
    w_f = jnp.pad(w_t[:, 3 * FOX_WIDTH:3 * FOX_WIDTH + n_fox], ((0, 0), (0, LANES - n_fox), (0, 0))).astype(BF16)
    b_f = jnp.pad(b_forget, ((0, 0), (0, LANES - n_fox)))
    w_bf = w_branch_fox.astype(BF16)
    w_bd = w_branch_diff.astype(BF16)
    w_o = w_out.astype(BF16)

    fk_c = cache_fox_k.transpose(0, 1, 3, 4, 2).reshape(depth * n_pool, FOX_WIDTH, page)
    fv_c = cache_fox_v.transpose(0, 1, 3, 4, 2).reshape(depth * n_pool, FOX_WIDTH, page)
    dk_c = cache_diff_k.transpose(0, 1, 3, 4, 5, 2).reshape(depth * n_pool, DIFF_WIDTH, page)
    dv_c = cache_diff_v.reshape(depth * n_pool, page * n_diff, 2 * HEAD_DIM)
    lf_c = cache_fox_logf.transpose(0, 1, 3, 2).reshape(depth * n_pool, n_fox, page)

    bias_p = _prompt_bias(rel_bias, tq, tq)
    bias_s, bias_s0 = _decode_bias(rel_bias, past_len, page)

    xp = x_prompt
    xs = x_sample.reshape(bs, d)
    stacked = ()
    outs_s = [[] for _ in range(5)]
    for l in range(depth):
        lam_init = _lambda_init(l)
        final = l == depth - 1
        g = norm_g[l][None]
        lvec = [a[l][None] for a in (lambda_q1, lambda_k1, lambda_q2, lambda_k2)]
        g_sub = diff_subln_g[l]

        (fqt, fkb, fvt, fz, dqt, dkb, dvt, dz, ga, gb, ct, c, *stacked) = _inproj_prompt(
            xp, g, w_main[l], w_f[l], b_f[l][:, None], tuple(stacked), l, depth, tm=tq)
        chunks = lambda a: a.reshape(b, s // tq, a.shape[1], tq)
        rows = lambda a: a.reshape(b, s, -1)
        n_streams = 2 * n_units
        cq = ct.reshape(b, n_fox // n_streams, n_streams, s)
        ck = c.reshape(b, s, n_fox // n_streams, n_streams).transpose(0, 2, 1, 3)
        o_f = _fox_attention(chunks(fqt), rows(fkb), chunks(fvt), cq, ck, tq=tq, n_units=n_units)
        o_d = _diff_attention(chunks(dqt), rows(dkb), chunks(dvt), bias_p, *lvec, g_sub[:, None],
                              tq=tq, n_units=n_units, lam_init=lam_init)
        xp = _merge(o_f.reshape(b * s, -1), fz, o_d.reshape(b * s, -1), dz, ga, gb, xp.reshape(b * s, d),
                    w_bf[l], w_bd[l], w_o[l], final_norm_g[None], tm=tq, final=final,
                    name="merge_prompt").reshape(b, s, d)

        sq, sk, sv, sz, sdq, sdk, sdv, sdz, sga, sgb, slf = _inproj_sample(xs, g, w_main[l], w_f[l], b_f[l][None])
        row3 = lambda a: a.reshape(bs, 1, -1)
        o_fs, o_ds = _decode_attention(
            page_table, l * n_pool, fk_c, lf_c, dk_c, fv_c, dv_c,
            row3(sq), row3(sdq), row3(sk), row3(sv), row3(sdk), row3(sdv), slf.reshape(bs, n_fox, 1),
            bias_s, bias_s0, *lvec, g_sub[None], n_slots=n_slots, lam_init=lam_init)
        xs = _merge(o_fs.reshape(bs, -1), sz, o_ds.reshape(bs, -1), sdz, sga, sgb, xs, w_bf[l], w_bd[l], w_o[l],
                    final_norm_g[None], tm=bs, final=final, name="merge_sample")
        for acc, a in zip(outs_s, (sk, sv, slf, sdk, sdv)):
            acc.append(a)

    pk, pv, plf, pdk, pdv = stacked
    sk_l, sv_l, slf_l, sdk_l, sdv_l = [jnp.stack(a) for a in outs_s]
    return (xp, xs.reshape(bs, 1, d),
            pk.reshape(depth, b, n_fox, HEAD_DIM, s).transpose(0, 1, 4, 2, 3),
            pv.reshape(depth, b, n_fox, HEAD_DIM, s).transpose(0, 1, 4, 2, 3),
            plf.transpose(0, 1, 3, 2),
            pdk.reshape(depth, b, n_diff, 2, HEAD_DIM, s).transpose(0, 1, 5, 2, 3, 4),
            pdv.reshape(depth, b, s, n_diff, 2 * HEAD_DIM),
            sk_l.reshape(depth, bs, 1, n_fox, HEAD_DIM), sv_l.reshape(depth, bs, 1, n_fox, HEAD_DIM),
            slf_l.reshape(depth, bs, 1, n_fox),
            sdk_l.reshape(depth, bs, 1, n_diff, 2, HEAD_DIM), sdv_l.reshape(depth, bs, 1, n_diff, 2 * HEAD_DIM))
```

```python
import functools
import math

import jax
import jax.numpy as jnp
from jax import lax
from jax.experimental import pallas as pl
from jax.experimental.pallas import tpu as pltpu

HEAD_DIM = 64
LANES = 128
FOX_WIDTH = 512
DIFF_WIDTH = 512
NUM_BUCKETS = 32
MAX_DISTANCE = 128
RMS_EPS = 1e-6
NEG_INF = -1e30
SCALE = HEAD_DIM ** -0.5
LOG2E = math.log2(math.e)
SUM_ROWS = 16
VMEM_LIMIT = 56 * 1024 * 1024

F32 = jnp.float32
BF16 = jnp.bfloat16

_NT = (((1,), (1,)), ((), ()))


def _lambda_init(layer):
    return 0.8 - 0.6 * math.exp(-0.3 * layer)


def _sigmoid(x):
    return 1.0 / (1.0 + jnp.exp(-x))


def _log_sigmoid(x):
    return jnp.minimum(x, 0.0) - jnp.log1p(jnp.exp(-jnp.abs(x)))


def _split3(x):
    h1 = x.astype(BF16)
    r1 = x - h1.astype(F32)
    h2 = r1.astype(BF16)
    r2 = r1 - h2.astype(F32)
    return h1, h2, r2.astype(BF16)


def _prefix_sums(x, n):
    u = lax.broadcasted_iota(jnp.int32, (n, n), 0)
    s = lax.broadcasted_iota(jnp.int32, (n, n), 1)
    tri = jnp.where(u <= s, 1.0, 0.0).astype(BF16)
    h1, h2, h3 = _split3(x)
    return (jnp.dot(h1, tri, preferred_element_type=F32) + jnp.dot(h2, tri, preferred_element_type=F32)
            + jnp.dot(h3, tri, preferred_element_type=F32))


def _t5_bucket(rel):
    n = jnp.maximum(rel, 0)
    max_exact = NUM_BUCKETS // 2
    nf = jnp.maximum(n, 1).astype(F32)
    large = max_exact + (jnp.log(nf / max_exact) / math.log(MAX_DISTANCE / max_exact)
                         * (NUM_BUCKETS - max_exact)).astype(jnp.int32)
    large = jnp.minimum(large, NUM_BUCKETS - 1)
    return jnp.where(n < max_exact, n, large)


def _lam_value(lq1, lk1, lq2, lk2, lam_init):
    a = jnp.sum(lq1 * lk1, axis=1, keepdims=True)
    b = jnp.sum(lq2 * lk2, axis=1, keepdims=True)
    return jnp.exp(a) - jnp.exp(b) + lam_init


def _rms_rows(x, g_ref):
    y = x * lax.rsqrt(jnp.mean(x * x, axis=-1, keepdims=True) + RMS_EPS)
    return y * g_ref[...]


_FQ, _FK, _FV, _FZ, _DQ, _DK, _DV, _DZ, _GA, _GB = (0, 512, 1024, 1536, 2048, 2560, 3072, 3584, 4096, 5120)


def _inproj_prompt_kernel(*refs, n_alias):
    x_ref, g_ref, w_ref, wf_ref, bf_ref = refs[:5]
    (fqt_ref, fk_ref, fvt_ref, fz_ref, dqt_ref, dk_ref, dvt_ref, dz_ref, ga_ref, gb_ref, ct_ref, c_ref,
     pk_ref, pv_ref, plf_ref, pdk_ref, pdv_ref, carry_ref) = refs[5 + n_alias:]
    hb = _rms_rows(x_ref[...], g_ref).astype(BF16)
    tm = hb.shape[0]

    def tr(r0, n):
        return lax.dot_general(w_ref[r0:r0 + n, :], hb, _NT, preferred_element_type=F32)

    def nat(r0, n):
        return lax.dot_general(hb, w_ref[r0:r0 + n, :], _NT, preferred_element_type=F32)

    fqt_ref[0] = (tr(_FQ, 512) * (SCALE * LOG2E)).astype(BF16)
    kt = tr(_FK, 512)
    pk_ref[...] = kt
    fk_ref[...] = kt.T.astype(BF16)
    vt = tr(_FV, 512)
    pv_ref[...] = vt
    fvt_ref[0] = vt.astype(BF16)
    fz_ref[...] = nat(_FZ, 512).astype(BF16)
    dqt_ref[0] = (tr(_DQ, 512) * (SCALE * LOG2E)).astype(BF16)
    kt = tr(_DK, 512)
    pdk_ref[...] = kt
    dk_ref[...] = kt.T.astype(BF16)
    v = nat(_DV, 512)
    pdv_ref[...] = v
    dvt_ref[0] = v.T.astype(BF16)
    dz_ref[...] = nat(_DZ, 512).astype(BF16)
    ga_ref[...] = nat(_GA, 1024).astype(BF16)
    gb_ref[...] = nat(_GB, 1024).astype(BF16)

    fft = lax.dot_general(wf_ref[...], hb, _NT, preferred_element_type=F32) + bf_ref[...]
    lft = _log_sigmoid(fft)
    plf_ref[...] = lft[:8]

    @pl.when(pl.program_id(1) == 0)
    def _():
        carry_ref[...] = jnp.zeros_like(carry_ref)

    ct = _prefix_sums(lft, tm) + carry_ref[...]
    carry_ref[...] = ct[:, tm - 1:tm]
    ct_ref[...] = ct[:8]
    c_ref[...] = ct.T[:, :8]


def _inproj_prompt(x3d, g, w_t, wf_t, bf_col, stacked, layer, depth, *, tm):
    b, s, d = x3d.shape
    tps = s // tm
    n_alias = len(stacked)
    im2 = lambda bi, i: (bi * tps + i, 0)
    im3 = lambda bi, i: (bi * tps + i, 0, 0)
    const = lambda shape: pl.BlockSpec(shape, lambda bi, i: (0,) * len(shape), pipeline_mode=pl.Buffered(1))
    m = b * s
    out_shape = [
        jax.ShapeDtypeStruct((m // tm, 512, tm), BF16),
        jax.ShapeDtypeStruct((m, 512), BF16),
        jax.ShapeDtypeStruct((m // tm, 512, tm), BF16),
        jax.ShapeDtypeStruct((m, 512), BF16),
        jax.ShapeDtypeStruct((m // tm, 512, tm), BF16),
        jax.ShapeDtypeStruct((m, 512), BF16),
        jax.ShapeDtypeStruct((m // tm, 512, tm), BF16),
        jax.ShapeDtypeStruct((m, 512), BF16),
        jax.ShapeDtypeStruct((m, 1024), BF16),
        jax.ShapeDtypeStruct((m, 1024), BF16),
        jax.ShapeDtypeStruct((b, 8, s), F32),
        jax.ShapeDtypeStruct((m, 8), F32),
        jax.ShapeDtypeStruct((depth, b, 512, s), F32),
        jax.ShapeDtypeStruct((depth, b, 512, s), F32),
        jax.ShapeDtypeStruct((depth, b, 8, s), F32),
        jax.ShapeDtypeStruct((depth, b, 512, s), F32),
        jax.ShapeDtypeStruct((depth, b, s, 512), F32),
    ]
    t_spec = pl.BlockSpec((1, 512, tm), im3)
    r_spec = lambda width: pl.BlockSpec((tm, width), im2)
    st_spec = lambda rows: pl.BlockSpec((None, None, rows, tm), lambda bi, i: (layer, bi, 0, i))
    out_specs = [t_spec, r_spec(512), t_spec, r_spec(512), t_spec, r_spec(512), t_spec, r_spec(512),
                 r_spec(1024), r_spec(1024),
                 pl.BlockSpec((None, 8, tm), lambda bi, i: (bi, 0, i)), r_spec(8),
                 st_spec(512), st_spec(512), st_spec(8), st_spec(512),
                 pl.BlockSpec((None, None, tm, 512), lambda bi, i: (layer, bi, i, 0))]
    in_specs = ([pl.BlockSpec((None, tm, d), lambda bi, i: (bi, i, 0)), const((1, d)), const(w_t.shape),
                 const(wf_t.shape), const(bf_col.shape)]
                + [pl.BlockSpec(memory_space=pl.ANY)] * n_alias)
    return pl.pallas_call(
        functools.partial(_inproj_prompt_kernel, n_alias=n_alias),
        grid=(b, tps),
        in_specs=in_specs,
        out_specs=out_specs,
        out_shape=out_shape,
        scratch_shapes=[pltpu.VMEM((LANES, 1), F32)],
        input_output_aliases={5 + k: 12 + k for k in range(n_alias)},
        compiler_params=pltpu.CompilerParams(dimension_semantics=("arbitrary", "arbitrary"),
                                             vmem_limit_bytes=VMEM_LIMIT),
        name="inproj_prompt",
    )(x3d, g, w_t, wf_t, bf_col, *stacked)


def _inproj_sample_kernel(x_ref, g_ref, w_ref, wf_ref, bf_ref,
                          fq_ref, fk_ref, fv_ref, fz_ref, dq_ref, dk_ref, dv_ref, dz_ref, ga_ref, gb_ref, lf_ref):
    hb = _rms_rows(x_ref[...], g_ref).astype(BF16)

    def nat(r0, n):
        return lax.dot_general(hb, w_ref[r0:r0 + n, :], _NT, preferred_element_type=F32)

    fq_ref[...] = nat(_FQ, 512) * SCALE
    fk_ref[...] = nat(_FK, 512)
    fv_ref[...] = nat(_FV, 512)
    fz_ref[...] = nat(_FZ, 512)
    dq_ref[...] = nat(_DQ, 512) * SCALE
    dk_ref[...] = nat(_DK, 512)
    dv_ref[...] = nat(_DV, 512)
    dz_ref[...] = nat(_DZ, 512)
    ga_ref[...] = nat(_GA, 1024)
    gb_ref[...] = nat(_GB, 1024)
    ff = lax.dot_general(hb, wf_ref[...], _NT, preferred_element_type=F32) + bf_ref[...]
    lf_ref[...] = _log_sigmoid(ff)[:, :8]


def _inproj_sample(x2d, g, w_t, wf_t, bf_row):
    m, d = x2d.shape
    full = lambda shape: pl.BlockSpec(shape, lambda i: (0,) * len(shape))
    widths = [512] * 8 + [1024, 1024, 8]
    return pl.pallas_call(
        _inproj_sample_kernel,
        grid=(1,),
        in_specs=[full((m, d)), full((1, d)), full(w_t.shape), full(wf_t.shape), full(bf_row.shape)],
        out_specs=[full((m, w)) for w in widths],
        out_shape=[jax.ShapeDtypeStruct((m, w), F32) for w in widths],
        compiler_params=pltpu.CompilerParams(dimension_semantics=("arbitrary",), vmem_limit_bytes=VMEM_LIMIT),
        name="inproj_sample",
    )(x2d, g, w_t, wf_t, bf_row)


def _bias_lookup(rb_ref, bucket, head):
    out = jnp.zeros(bucket.shape, F32)
    for b in range(NUM_BUCKETS):
        out = jnp.where(bucket == b, rb_ref[b, head], out)
    return out


def _prompt_bias_kernel(rb_ref, out_ref, *, tq, tk):
    h = pl.program_id(0)
    key = lax.broadcasted_iota(jnp.int32, (tk, tq), 0)
    qry = lax.broadcasted_iota(jnp.int32, (tk, tq), 1)
    for t in range(3):
        bucket = _t5_bucket(qry - key + t * tk)
        out_ref[0, t] = _bias_lookup(rb_ref, bucket, h) * LOG2E


def _prompt_bias(rel_bias, tq, tk):
    return pl.pallas_call(
        functools.partial(_prompt_bias_kernel, tq=tq, tk=tk),
        grid=(4,),
        in_specs=[pl.BlockSpec(memory_space=pltpu.SMEM)],
        out_specs=pl.BlockSpec((1, 3, tk, tq), lambda h: (h, 0, 0, 0)),
        out_shape=jax.ShapeDtypeStruct((4, 3, tk, tq), F32),
        name="prompt_bias",
    )(rel_bias)


def _decode_bias_kernel(rb_ref, out_ref, self_ref, *, past_len, page):
    p = pl.program_id(0)
    pos = p * page + lax.broadcasted_iota(jnp.int32, (8, page), 1)
    row = lax.broadcasted_iota(jnp.int32, (8, page), 0)
    bucket = _t5_bucket(past_len - pos)
    bucket0 = _t5_bucket(jnp.zeros((8, page), jnp.int32))
    acc = jnp.zeros((8, page), F32)
    acc0 = jnp.zeros((8, page), F32)
    for hh in range(4):
        sel = (row // 2) == hh
        acc = jnp.where(sel, _bias_lookup(rb_ref, bucket, hh), acc)
        acc0 = jnp.where(sel, _bias_lookup(rb_ref, bucket0, hh), acc0)
    out_ref[0] = acc
    self_ref[...] = acc0


def _decode_bias(rel_bias, past_len, page):
    n_pages = past_len // page
    return pl.pallas_call(
        functools.partial(_decode_bias_kernel, past_len=past_len, page=page),
        grid=(n_pages,),
        in_specs=[pl.BlockSpec(memory_space=pltpu.SMEM)],
        out_specs=[pl.BlockSpec((1, 8, page), lambda p: (p, 0, 0)), pl.BlockSpec((8, page), lambda p: (0, 0))],
        out_shape=[jax.ShapeDtypeStruct((n_pages, 8, page), F32), jax.ShapeDtypeStruct((8, page), F32)],
        name="decode_bias",
    )(rel_bias)


def _attn_chunk(j, masked, qs, k_ref, vt_ref, sub_term, add_term, col_terms, m_sc, acc_sc, tk):
    start = pl.multiple_of(j * tk, tk)
    tq = qs[0].shape[1]
    if masked:
        key = lax.broadcasted_iota(jnp.int32, (tk, tq), 0)
        qry = lax.broadcasted_iota(jnp.int32, (tk, tq), 1)
        keep = key <= qry
    ts = []
    for n, qn in enumerate(qs):
        u = n // 2
        ks = k_ref[0, pl.ds(start, tk), u * LANES:(u + 1) * LANES]
        t = jnp.dot(ks, qn, preferred_element_type=F32)
        if sub_term is not None:
            t = t - sub_term(n, start)
        if add_term is not None:
            t = t + add_term(n)
        if masked:
            t = jnp.where(keep, t, NEG_INF)
        ts.append(t)
    ps, alphas = [], []
    for n, t in enumerate(ts):
        ct = col_terms[n]
        mcol = jnp.max(t, axis=0, keepdims=True)
        if ct is not None:
            mcol = mcol + ct
        m_old = m_sc[n]
        m_new = jnp.maximum(m_old, mcol)
        shift = m_new if ct is None else m_new - ct
        ps.append(jnp.exp2(t - shift).astype(BF16))
        alphas.append(jnp.exp2(m_old - m_new))
        m_sc[n] = m_new
    ones = jnp.ones((SUM_ROWS, tk), BF16)
    for n, p in enumerate(ps):
        u = n // 2
        vts = jnp.concatenate([vt_ref[0, j, u * LANES:(u + 1) * LANES, :], ones], axis=0)
        acc_sc[n] = alphas[n] * acc_sc[n] + jnp.dot(vts, p, preferred_element_type=F32)


def _attn_init(m_sc, acc_sc):
    m_sc[...] = jnp.full(m_sc.shape, -jnp.inf, F32)
    acc_sc[...] = jnp.zeros_like(acc_sc)


def _attn_result(acc_sc, n):
    return acc_sc[n, :LANES, :] / acc_sc[n, LANES:LANES + 1, :]


def _split_q(qt_ref, n_units):
    qs = []
    for u in range(n_units):
        qt = qt_ref[0, 0, u * LANES:(u + 1) * LANES, :]
        row = lax.broadcasted_iota(jnp.int32, qt.shape, 0)
        zero = jnp.zeros_like(qt)
        qs += [jnp.where(row < HEAD_DIM, qt, zero), jnp.where(row >= HEAD_DIM, qt, zero)]
    return qs


def _fox_attn_kernel(qt_ref, k_ref, vt_ref, cq_ref, ck_ref, o_ref, ckb_sc, m_sc, acc_sc, *, tk, n_units):
    qi = pl.program_id(2)
    n_streams = 2 * n_units

    @pl.when(qi == 0)
    def _():
        ck = ck_ref[0, 0] * LOG2E
        for n in range(n_streams):
            ckb_sc[n] = jnp.broadcast_to(ck[:, n:n + 1], ckb_sc.shape[1:])

    qs = _split_q(qt_ref, n_units)
    cq = cq_ref[0, 0] * LOG2E
    col_terms = [cq[n:n + 1, :] for n in range(n_streams)]
    _attn_init(m_sc, acc_sc)

    def key_decay(n, start):
        ckn = ckb_sc[n, pl.ds(start, tk), :]
        return jnp.concatenate([ckn] * (qs[0].shape[1] // LANES), axis=1)

    def step(j, masked):
        _attn_chunk(j, masked, qs, k_ref, vt_ref, key_decay, None, col_terms, m_sc, acc_sc, tk)

    def body(j, carry):
        step(j, False)
        return carry

    lax.fori_loop(0, qi, body, 0)
    step(qi, True)
    row = lax.broadcasted_iota(jnp.int32, (LANES, qs[0].shape[1]), 0)
    for u in range(n_units):
        ot = jnp.where(row < HEAD_DIM, _attn_result(acc_sc, 2 * u), _attn_result(acc_sc, 2 * u + 1))
        o_ref[0, :, u * LANES:(u + 1) * LANES] = ot.T.astype(o_ref.dtype)


def _diff_attn_kernel(qt_ref, k_ref, vt_ref, bias_ref, lq1_ref, lk1_ref, lq2_ref, lk2_ref, g_ref,
                      o_ref, m_sc, acc_sc, *, tk, n_units, lam_init):
    qi = pl.program_id(2)
    n_streams = 2 * n_units
    qs = _split_q(qt_ref, n_units)
    _attn_init(m_sc, acc_sc)

    def step(j, tile, masked):
        _attn_chunk(j, masked, qs, k_ref, vt_ref, None, lambda n: bias_ref[n // 2, tile], [None] * n_streams,
                    m_sc, acc_sc, tk)

    def body(j, carry):
        step(j, 2, False)
        return carry

    lax.fori_loop(0, qi - 1, body, 0)

    @pl.when(qi >= 1)
    def _():
        step(qi - 1, 1, False)

    step(qi, 0, True)
    lam = _lam_value(lq1_ref[...], lk1_ref[...], lq2_ref[...], lk2_ref[...], lam_init)
    for u in range(n_units):
        ot = _attn_result(acc_sc, 2 * u) - lam * _attn_result(acc_sc, 2 * u + 1)
        y = ot * lax.rsqrt(jnp.mean(ot * ot, axis=0, keepdims=True) + RMS_EPS)
        o_ref[0, :, u * LANES:(u + 1) * LANES] = ((y * g_ref[...]) * (1.0 - lam_init)).T.astype(o_ref.dtype)


def _attn_common(b, s, tq, n_units):
    width = n_units * LANES
    nq = s // tq
    grid = (b, FOX_WIDTH // width, nq)
    qt_spec = pl.BlockSpec((1, 1, width, tq), lambda bi, g, qi: (bi, qi, g, 0))
    k_spec = pl.BlockSpec((1, s, width), lambda bi, g, qi: (bi, 0, g))
    vt_spec = pl.BlockSpec((1, nq, width, tq), lambda bi, g, qi: (bi, 0, g, 0))
    o_spec = pl.BlockSpec((1, tq, width), lambda bi, g, qi: (bi, qi, g))
    n_streams = 2 * n_units
    scratch = [pltpu.VMEM((n_streams, 1, tq), F32), pltpu.VMEM((n_streams, LANES + SUM_ROWS, tq), F32)]
    params = pltpu.CompilerParams(dimension_semantics=("arbitrary", "arbitrary", "arbitrary"),
                                  vmem_limit_bytes=VMEM_LIMIT)
    return grid, qt_spec, k_spec, vt_spec, o_spec, scratch, params


def _fox_attention(qt, kb, vt, cq, ck, *, tq, n_units):
    b, s, _ = kb.shape
    grid, qt_spec, k_spec, vt_spec, o_spec, scratch, params = _attn_common(b, s, tq, n_units)
    n_streams = 2 * n_units
    return pl.pallas_call(
        functools.partial(_fox_attn_kernel, tk=tq, n_units=n_units),
        grid=grid,
        in_specs=[qt_spec, k_spec, vt_spec,
                  pl.BlockSpec((1, 1, n_streams, tq), lambda bi, g, qi: (bi, g, 0, qi)),
                  pl.BlockSpec((1, 1, s, n_streams), lambda bi, g, qi: (bi, g, 0, 0))],
        out_specs=o_spec,
        out_shape=jax.ShapeDtypeStruct((b, s, FOX_WIDTH), BF16),
        scratch_shapes=[pltpu.VMEM((n_streams, s, LANES), F32)] + scratch,
        compiler_params=params,
        name="fox_attention",
    )(qt, kb, vt, cq, ck)


def _diff_attention(qt, kb, vt, bias, lq1, lk1, lq2, lk2, g_col, *, tq, n_units, lam_init):
    b, s, _ = kb.shape
    grid, qt_spec, k_spec, vt_spec, o_spec, scratch, params = _attn_common(b, s, tq, n_units)
    full = lambda a: pl.BlockSpec(a.shape, lambda bi, g, qi: (0,) * a.ndim)
    return pl.pallas_call(
        functools.partial(_diff_attn_kernel, tk=tq, n_units=n_units, lam_init=lam_init),
        grid=grid,
        in_specs=[qt_spec, k_spec, vt_spec,
                  pl.BlockSpec((n_units, 3, tq, tq), lambda bi, g, qi: (g, 0, 0, 0)),
                  full(lq1), full(lk1), full(lq2), full(lk2), full(g_col)],
        out_specs=o_spec,
        out_shape=jax.ShapeDtypeStruct((b, s, DIFF_WIDTH), BF16),
        scratch_shapes=scratch,
        compiler_params=params,
        name="diff_attention",
    )(qt, kb, vt, bias, lq1, lk1, lq2, lk2, g_col)


def _merge_kernel(of_ref, fz_ref, od_ref, dz_ref, ga_ref, gb_ref, x_ref, wbf_ref, wbd_ref, wo_ref, g_ref,
                  out_ref, *, final):
    fz = fz_ref[...].astype(F32)
    dz = dz_ref[...].astype(F32)
    a = (of_ref[...].astype(F32) * (fz * _sigmoid(fz))).astype(BF16)
    d = (od_ref[...].astype(F32) * (dz * _sigmoid(dz))).astype(BF16)
    y_fox = jnp.dot(a, wbf_ref[...], preferred_element_type=F32)
    y_diff = jnp.dot(d, wbd_ref[...], preferred_element_type=F32)
    m = _sigmoid(ga_ref[...].astype(F32)) * y_fox + _sigmoid(gb_ref[...].astype(F32)) * y_diff
    x = x_ref[...] + jnp.dot(m.astype(BF16), wo_ref[...], preferred_element_type=F32)
    if final:
        x = _rms_rows(x, g_ref)
    out_ref[...] = x


def _merge(o_f, fz, o_d, dz, ga, gb, x2d, w_bf, w_bd, w_o, g_final, *, tm, final, name):
    m, d = x2d.shape
    row_spec = lambda width: pl.BlockSpec((tm, width), lambda i: (i, 0))
    const = lambda shape: pl.BlockSpec(shape, lambda i: (0, 0), pipeline_mode=pl.Buffered(1))
    return pl.pallas_call(
        functools.partial(_merge_kernel, final=final),
        grid=(m // tm,),
        in_specs=[row_spec(512), row_spec(512), row_spec(512), row_spec(512), row_spec(d), row_spec(d),
                  row_spec(d), const(w_bf.shape), const(w_bd.shape), const(w_o.shape), const((1, d))],
        out_specs=row_spec(d),
        out_shape=jax.ShapeDtypeStruct((m, d), F32),
        compiler_params=pltpu.CompilerParams(dimension_semantics=("arbitrary",), vmem_limit_bytes=VMEM_LIMIT),
        name=name,
    )(o_f, fz, o_d, dz, ga, gb, x2d, w_bf, w_bd, w_o, g_final)


def _decode_kernel(pt_ref, fk_hbm, lf_hbm, dk_hbm, fv_hbm, dv_hbm,
                   fq_ref, dq_ref, fkn_ref, fvn_ref, dkn_ref, dvn_ref, lfn_ref, bias_ref, bias0_ref,
                   lq1_ref, lk1_ref, lq2_ref, lk2_ref, g_ref,
                   of_ref, od_ref,
                   buf_a, buf_b, buf_lf, sem, sf_sc, sd_sc, w4_sc, *, base, n_slots, n_pages, page, lam_init):
    b = pl.program_id(0)
    n_chunks = n_pages // n_slots
    width = fq_ref.shape[-1]
    n_diff = width // LANES
    hv = lax.broadcasted_iota(jnp.int32, (8, width), 0)
    lane = lax.broadcasted_iota(jnp.int32, (8, width), 1)
    own = (lane // HEAD_DIM) == hv
    a_f = jnp.where(own, fq_ref[0], 0.0).astype(BF16)
    a_d = jnp.where(own, dq_ref[0], 0.0).astype(BF16)

    def page_copies(kinds, chunk, slot):
        out = []
        for i in range(n_slots):
            pg = base + pt_ref[b, chunk * n_slots + i]
            for src, dst, col in kinds:
                out.append(pltpu.make_async_copy(src.at[pg], dst.at[slot, i], sem.at[slot, col]))
        return out

    k_kinds = ((fk_hbm, buf_a, 0), (dk_hbm, buf_b, 1), (lf_hbm, buf_lf, 2))
    v_kinds = ((fv_hbm, buf_a, 0), (dv_hbm, buf_b, 1))

    def pipelined(kinds, body, init):
        for cp in page_copies(kinds, 0, 0):
            cp.start()

        def step(chunk, carry):
            slot = chunk % 2

            @pl.when(chunk + 1 < n_chunks)
            def _():
                for cp in page_copies(kinds, chunk + 1, 1 - slot):
                    cp.start()

            for cp in page_copies(kinds, chunk, slot):
                cp.wait()
            return body(chunk, slot, carry)

        return lax.fori_loop(0, n_chunks, step, init)

    u = lax.broadcasted_iota(jnp.int32, (page, page), 0)
    s = lax.broadcasted_iota(jnp.int32, (page, page), 1)
    tri = jnp.where(u <= s, 1.0, 0.0).astype(BF16)

    def key_chunk(chunk, slot, run):
        lf = buf_lf[slot].reshape(n_slots * 8, page)
        h1, h2, h3 = _split3(lf)
        local = (jnp.dot(h1, tri, preferred_element_type=F32) + jnp.dot(h2, tri, preferred_element_type=F32)
                 + jnp.dot(h3, tri, preferred_element_type=F32))
        for i in range(n_slots):
            pg = chunk * n_slots + i
            pre = local[8 * i:8 * i + 8] + run
            run = jnp.broadcast_to(pre[:, page - 1:page], run.shape)
            qk = jnp.dot(a_f, buf_a[slot, i].astype(BF16), preferred_element_type=F32)
            sf_sc[pg] = qk - pre
            sd_sc[pg] = jnp.dot(a_d, buf_b[slot, i].astype(BF16), preferred_element_type=F32) + bias_ref[pg]
        return run

    run = pipelined(k_kinds, key_chunk, jnp.zeros((8, page), F32))

    for cp in page_copies(v_kinds, 0, 0):
        cp.start()

    def softmax_rows(logits, self_logit):
        mx = jnp.max(jnp.max(logits, axis=0), axis=1, keepdims=True)
        mx = jnp.maximum(mx, self_logit)
        p = jnp.exp(logits - mx[None])
        p_self = jnp.exp(self_logit - mx)
        den = jnp.sum(jnp.sum(p, axis=0), axis=1, keepdims=True) + p_self
        return p / den[None], p_self / den

    def self_logit(a, kn_ref):
        kn = kn_ref[0].astype(BF16).astype(F32)
        return jnp.sum(a.astype(F32) * kn, axis=1, keepdims=True)

    c_new = run[:, 0:1] + lfn_ref[0]
    w, wf_self = softmax_rows(sf_sc[...] + c_new[None], self_logit(a_f, fkn_ref) + (c_new - c_new))
    sf_sc[...] = w
    w, wd_self = softmax_rows(sd_sc[...], self_logit(a_d, dkn_ref) + bias0_ref[:, 0:1])
    lam = _lam_value(lq1_ref[...], lk1_ref[...], lq2_ref[...], lk2_ref[...], lam_init)
    rows = n_pages * 8
    w2 = w.reshape(rows, page)
    row = lax.broadcasted_iota(jnp.int32, (rows, page), 0)
    wc = jnp.where(row % 2 == 0, w2 - lam * pltpu.roll(w2, rows - 1, 0), 0.0)
    wd8 = jnp.broadcast_to(wd_self, (8, page))
    wd_self = jnp.where(row[:8] % 2 == 0, wd8 - lam * pltpu.roll(wd8, 7, 0), 0.0)[:, 0:1]
    pos = lax.broadcasted_iota(jnp.int32, (page, page * n_diff), 0)
    col = lax.broadcasted_iota(jnp.int32, (page, page * n_diff), 1)
    rep = jnp.where(col // n_diff == pos, 1.0, 0.0).astype(BF16)
    spread = jnp.dot(wc.astype(BF16), rep, preferred_element_type=F32)
    rowx = lax.broadcasted_iota(jnp.int32, spread.shape, 0)
    colx = lax.broadcasted_iota(jnp.int32, spread.shape, 1)
    w4_sc[...] = jnp.where((colx % n_diff) * 2 == rowx % 8, spread, 0.0).reshape(w4_sc.shape)

    def value_chunk(chunk, slot, carry):
        acc_f, acc_d = carry
        for i in range(n_slots):
            pg = chunk * n_slots + i
            acc_f = acc_f + lax.dot_general(sf_sc[pg].astype(BF16), buf_a[slot, i].astype(BF16), _NT,
                                            preferred_element_type=F32)
            acc_d = acc_d + jnp.dot(w4_sc[pg].astype(BF16), buf_b[slot, i].astype(BF16),
                                    preferred_element_type=F32)
        return acc_f, acc_d

    def value_pipeline():
        def step(chunk, carry):
            slot = chunk % 2

            @pl.when(chunk + 1 < n_chunks)
            def _():
                for cp in page_copies(v_kinds, chunk + 1, 1 - slot):
                    cp.start()

            for cp in page_copies(v_kinds, chunk, slot):
                cp.wait()
            return value_chunk(chunk, slot, carry)

        return lax.fori_loop(0, n_chunks, step, (jnp.zeros((8, width), F32), jnp.zeros((8, LANES), F32)))

    acc_f, acc_d = value_pipeline()

    acc = acc_f + wf_self * fvn_ref[0]
    of_ref[0] = jnp.sum(jnp.where(own, acc, 0.0), axis=0, keepdims=True)
    dvn = dvn_ref[0]
    zero = jnp.zeros((1, LANES), F32)
    vn8 = jnp.concatenate([piece for h in range(n_diff)
                           for piece in (dvn[:, h * LANES:(h + 1) * LANES], zero)], axis=0)
    od8 = acc_d + wd_self * vn8
    y8 = od8 * lax.rsqrt(jnp.mean(od8 * od8, axis=1, keepdims=True) + RMS_EPS)
    y8 = (y8 * g_ref[...]) * (1.0 - lam_init)
    od_ref[0] = jnp.concatenate([y8[2 * h:2 * h + 1] for h in range(n_diff)], axis=1)


def _decode_attention(page_table, base, fk, lft, dk, fv, dv, fq, dq, fkn, fvn, dkn, dvn, lfn, bias, bias0,
                      lq1, lk1, lq2, lk2, g, *, n_slots, lam_init):
    bs, n_pages = page_table.shape
    width, page = fk.shape[1], fk.shape[2]
    assert dv.shape[1:] == fk.shape[1:] and n_pages % n_slots == 0

    row_spec = pl.BlockSpec((1, 1, width), lambda b, pt: (b, 0, 0))
    full = lambda a: pl.BlockSpec(a.shape, lambda b, pt: (0,) * a.ndim)
    in_specs = ([pl.BlockSpec(memory_space=pl.ANY)] * 5
                + [row_spec] * 6
                + [pl.BlockSpec((1, 8, 1), lambda b, pt: (b, 0, 0)),
                   full(bias), full(bias0), full(lq1), full(lk1), full(lq2), full(lk2), full(g)])
    scratch = [pltpu.VMEM((2, n_slots, width, page), F32), pltpu.VMEM((2, n_slots, width, page), F32),
               pltpu.VMEM((2, n_slots, 8, page), F32), pltpu.SemaphoreType.DMA((2, 3)),
               pltpu.VMEM((n_pages, 8, page), F32), pltpu.VMEM((n_pages, 8, page), F32),
               pltpu.VMEM((n_pages, 8, dv.shape[1]), F32)]
    grid_spec = pltpu.PrefetchScalarGridSpec(
        num_scalar_prefetch=1, grid=(bs,), in_specs=in_specs,
        out_specs=[row_spec, row_spec], scratch_shapes=scratch)
    return pl.pallas_call(
        functools.partial(_decode_kernel, base=base, n_slots=n_slots, n_pages=n_pages, page=page,
                          lam_init=lam_init),
        grid_spec=grid_spec,
        out_shape=[jax.ShapeDtypeStruct((bs, 1, width), F32), jax.ShapeDtypeStruct((bs, 1, width), F32)],
        compiler_params=pltpu.CompilerParams(dimension_semantics=("arbitrary",), vmem_limit_bytes=VMEM_LIMIT),
        name="decode_attention",
    )(page_table, fk, lft, dk, fv, dv, fq, dq, fkn, fvn, dkn, dvn, lfn, bias, bias0, lq1, lk1, lq2, lk2, g)


def kernel(x_prompt, x_sample, cache_fox_k, cache_fox_v, cache_fox_logf, cache_diff_k, cache_diff_v, page_table,
           norm_g, w_in, b_forget, lambda_q1, lambda_k1, lambda_q2, lambda_k2, diff_subln_g, w_branch_fox,
           w_branch_diff, w_out, rel_bias, final_norm_g):
    depth = w_in.shape[0]
    b, s, d = x_prompt.shape
    bs = x_sample.shape[0]
    n_pool, page = cache_fox_k.shape[1], cache_fox_k.shape[2]
    n_pages = page_table.shape[1]
    past_len = n_pages * page
    n_fox = FOX_WIDTH // HEAD_DIM
    n_diff = DIFF_WIDTH // (2 * HEAD_DIM)
    tq = 256
    n_units = 4
    n_slots = 4

    w_t = jnp.swapaxes(w_in, 1, 2)
    w_main = jnp.concatenate([w_t[:, :3 * FOX_WIDTH], w_t[:, 3 * FOX_WIDTH + n_fox:]], axis=1).astype(BF16)
    w_f = jnp.pad(w_t[:, 3 * FOX_WIDTH:3 * FOX_WIDTH + n_fox], ((0, 0), (0, LANES - n_fox), (0, 0))).astype(BF16)
    b_f = jnp.pad(b_forget, ((0, 0), (0, LANES - n_fox)))
    w_bf = w_branch_fox.astype(BF16)
    w_bd = w_branch_diff.astype(BF16)
    w_o = w_out.astype(BF16)

    fk_c = cache_fox_k.transpose(0, 1, 3, 4, 2).reshape(depth * n_pool, FOX_WIDTH, page)
    fv_c = cache_fox_v.transpose(0, 1, 3, 4, 2).reshape(depth * n_pool, FOX_WIDTH, page)
    dk_c = cache_diff_k.transpose(0, 1, 3, 4, 5, 2).reshape(depth * n_pool, DIFF_WIDTH, page)
    dv_c = cache_diff_v.reshape(depth * n_pool, page * n_diff, 2 * HEAD_DIM)
    lf_c = cache_fox_logf.transpose(0, 1, 3, 2).reshape(depth * n_pool, n_fox, page)

    bias_p = _prompt_bias(rel_bias, tq, tq)
    bias_s, bias_s0 = _decode_bias(rel_bias, past_len, page)

    xp = x_prompt
    xs = x_sample.reshape(bs, d)
    stacked = (jnp.zeros((depth, b, FOX_WIDTH, s), F32), jnp.zeros((depth, b, FOX_WIDTH, s), F32),
               jnp.zeros((depth, b, n_fox, s), F32), jnp.zeros((depth, b, DIFF_WIDTH, s), F32),
               jnp.zeros((depth, b, s, DIFF_WIDTH), F32))
    outs_s = [[] for _ in range(5)]
    for l in range(depth):
        lam_init = _lambda_init(l)
        final = l == depth - 1
        g = norm_g[l][None]
        lvec = [a[l][None] for a in (lambda_q1, lambda_k1, lambda_q2, lambda_k2)]
        g_sub = diff_subln_g[l]

        (fqt, fkb, fvt, fz, dqt, dkb, dvt, dz, ga, gb, ct, c, *stacked) = _inproj_prompt(
            xp, g, w_main[l], w_f[l], b_f[l][:, None], tuple(stacked), l, depth, tm=tq)
        chunks = lambda a: a.reshape(b, s // tq, a.shape[1], tq)
        rows = lambda a: a.reshape(b, s, -1)
        n_streams = 2 * n_units
        cq = ct.reshape(b, n_fox // n_streams, n_streams, s)
        ck = c.reshape(b, s, n_fox // n_streams, n_streams).transpose(0, 2, 1, 3)
        o_f = _fox_attention(chunks(fqt), rows(fkb), chunks(fvt), cq, ck, tq=tq, n_units=n_units)
        o_d = _diff_attention(chunks(dqt), rows(dkb), chunks(dvt), bias_p, *lvec, g_sub[:, None],
                              tq=tq, n_units=n_units, lam_init=lam_init)
        xp = _merge(o_f.reshape(b * s, -1), fz, o_d.reshape(b * s, -1), dz, ga, gb, xp.reshape(b * s, d),
                    w_bf[l], w_bd[l], w_o[l], final_norm_g[None], tm=tq, final=final,
                    name="merge_prompt").reshape(b, s, d)

        sq, sk, sv, sz, sdq, sdk, sdv, sdz, sga, sgb, slf = _inproj_sample(xs, g, w_main[l], w_f[l], b_f[l][None])
        row3 = lambda a: a.reshape(bs, 1, -1)
        o_fs, o_ds = _decode_attention(
            page_table, l * n_pool, fk_c, lf_c, dk_c, fv_c, dv_c,
            row3(sq), row3(sdq), row3(sk), row3(sv), row3(sdk), row3(sdv), slf.reshape(bs, n_fox, 1),
            bias_s, bias_s0, *lvec, g_sub[None], n_slots=n_slots, lam_init=lam_init)
        xs = _merge(o_fs.reshape(bs, -1), sz, o_ds.reshape(bs, -1), sdz, sga, sgb, xs, w_bf[l], w_bd[l], w_o[l],
                    final_norm_g[None], tm=bs, final=final, name="merge_sample")
        for acc, a in zip(outs_s, (sk, sv, slf, sdk, sdv)):
            acc.append(a)

    pk, pv, plf, pdk, pdv = stacked
    sk_l, sv_l, slf_l, sdk_l, sdv_l = [jnp.stack(a) for a in outs_s]
    return (xp, xs.reshape(bs, 1, d),
            pk.reshape(depth, b, n_fox, HEAD_DIM, s).transpose(0, 1, 4, 2, 3),
            pv.reshape(depth, b, n_fox, HEAD_DIM, s).transpose(0, 1, 4, 2, 3),
            plf.transpose(0, 1, 3, 2),
            pdk.reshape(depth, b, n_diff, 2, HEAD_DIM, s).transpose(0, 1, 5, 2, 3, 4),
            pdv.reshape(depth, b, s, n_diff, 2 * HEAD_DIM),
            sk_l.reshape(depth, bs, 1, n_fox, HEAD_DIM), sv_l.reshape(depth, bs, 1, n_fox, HEAD_DIM),
            slf_l.reshape(depth, bs, 1, n_fox),
            sdk_l.reshape(depth, bs, 1, n_diff, 2, HEAD_DIM), sdv_l.reshape(depth, bs, 1, n_diff, 2 * HEAD_DIM))
```

```python
import functools
import math

import jax
import jax.numpy as jnp
from jax import lax
from jax.experimental import pallas as pl
from jax.experimental.pallas import tpu as pltpu

HEAD_DIM = 64
LANES = 128
FOX_WIDTH = 512
DIFF_WIDTH = 512
NUM_BUCKETS = 32
MAX_DISTANCE = 128
RMS_EPS = 1e-6
NEG_INF = -1e30
SCALE = HEAD_DIM ** -0.5
LOG2E = math.log2(math.e)
SUM_ROWS = 16
VMEM_LIMIT = 56 * 1024 * 1024

F32 = jnp.float32
BF16 = jnp.bfloat16

_NT = (((1,), (1,)), ((), ()))


def _lambda_init(layer):
    return 0.8 - 0.6 * math.exp(-0.3 * layer)


def _sigmoid(x):
    return 1.0 / (1.0 + jnp.exp(-x))


def _log_sigmoid(x):
    return jnp.minimum(x, 0.0) - jnp.log1p(jnp.exp(-jnp.abs(x)))


def _split3(x):
    h1 = x.astype(BF16)
    r1 = x - h1.astype(F32)
    h2 = r1.astype(BF16)
    r2 = r1 - h2.astype(F32)
    return h1, h2, r2.astype(BF16)


def _prefix_sums(x, n):
    u = lax.broadcasted_iota(jnp.int32, (n, n), 0)
    s = lax.broadcasted_iota(jnp.int32, (n, n), 1)
    tri = jnp.where(u <= s, 1.0, 0.0).astype(BF16)
    h1, h2, h3 = _split3(x)
    return (jnp.dot(h1, tri, preferred_element_type=F32) + jnp.dot(h2, tri, preferred_element_type=F32)
            + jnp.dot(h3, tri, preferred_element_type=F32))


def _t5_bucket(rel):
    n = jnp.maximum(rel, 0)
    max_exact = NUM_BUCKETS // 2
    nf = jnp.maximum(n, 1).astype(F32)
    large = max_exact + (jnp.log(nf / max_exact) / math.log(MAX_DISTANCE / max_exact)
                         * (NUM_BUCKETS - max_exact)).astype(jnp.int32)
    large = jnp.minimum(large, NUM_BUCKETS - 1)
    return jnp.where(n < max_exact, n, large)


def _lam_value(lq1, lk1, lq2, lk2, lam_init):
    a = jnp.sum(lq1 * lk1, axis=1, keepdims=True)
    b = jnp.sum(lq2 * lk2, axis=1, keepdims=True)
    return jnp.exp(a) - jnp.exp(b) + lam_init


def _layer_block(a, layer, single_buffer=False):
    idx = (layer,) + (0,) * (a.ndim - 1)
    kwargs = dict(pipeline_mode=pl.Buffered(1)) if single_buffer else {}
    return pl.BlockSpec((None,) + a.shape[1:], lambda *_: idx, **kwargs)


def _rms_rows(x, g_ref):
    y = x * lax.rsqrt(jnp.mean(x * x, axis=-1, keepdims=True) + RMS_EPS)
    return y * g_ref[...]


_FQ, _FK, _FV, _FZ, _DQ, _DK, _DV, _DZ, _GA, _GB = (0, 512, 1024, 1536, 2048, 2560, 3072, 3584, 4096, 5120)


def _inproj_prompt_kernel(*refs, n_alias):
    x_ref, g_ref, w_ref, wf_ref, bf_ref = refs[:5]
    (fqt_ref, fk_ref, fvt_ref, fz_ref, dqt_ref, dk_ref, dvt_ref, dz_ref, ga_ref, gb_ref, ct_ref, c_ref,
     pk_ref, pv_ref, plf_ref, pdk_ref, pdv_ref, carry_ref) = refs[5 + n_alias:]
    hb = _rms_rows(x_ref[...], g_ref).astype(BF16)
    tm = hb.shape[0]

    def tr(r0, n):
        return lax.dot_general(w_ref[r0:r0 + n, :], hb, _NT, preferred_element_type=F32)

    def nat(r0, n):
        return lax.dot_general(hb, w_ref[r0:r0 + n, :], _NT, preferred_element_type=F32)

    fqt_ref[0] = (tr(_FQ, 512) * (SCALE * LOG2E)).astype(BF16)
    kt = tr(_FK, 512)
    pk_ref[...] = kt
    fk_ref[...] = kt.T.astype(BF16)
    vt = tr(_FV, 512)
    pv_ref[...] = vt
    fvt_ref[0] = vt.astype(BF16)
    fz_ref[...] = nat(_FZ, 512).astype(BF16)
    dqt_ref[0] = (tr(_DQ, 512) * (SCALE * LOG2E)).astype(BF16)
    kt = tr(_DK, 512)
    pdk_ref[...] = kt
    dk_ref[...] = kt.T.astype(BF16)
    v = nat(_DV, 512)
    n_dv = v.shape[1] // LANES
    for h in range(n_dv):
        pdv_ref[pl.ds(h, tm, stride=n_dv), :] = v[:, h * LANES:(h + 1) * LANES]
    dvt_ref[0] = v.T.astype(BF16)
    dz_ref[...] = nat(_DZ, 512).astype(BF16)
    ga_ref[...] = nat(_GA, 1024).astype(BF16)
    gb_ref[...] = nat(_GB, 1024).astype(BF16)

    fft = lax.dot_general(wf_ref[...], hb, _NT, preferred_element_type=F32) + bf_ref[...]
    lft = _log_sigmoid(fft)
    plf_ref[...] = lft[:8]

    @pl.when(pl.program_id(1) == 0)
    def _():
        carry_ref[...] = jnp.zeros_like(carry_ref)

    ct = _prefix_sums(lft, tm) + carry_ref[...]
    carry_ref[...] = ct[:, tm - 1:tm]
    ct_ref[...] = ct[:8]
    c_ref[...] = ct.T[:, :8]


def _inproj_prompt(x3d, g, w_t, wf_t, bf_col, stacked, layer, depth, *, tm):
    b, s, d = x3d.shape
    tps = s // tm
    n_alias = len(stacked)
    im2 = lambda bi, i: (bi * tps + i, 0)
    im3 = lambda bi, i: (bi * tps + i, 0, 0)
    m = b * s
    out_shape = [
        jax.ShapeDtypeStruct((m // tm, 512, tm), BF16),
        jax.ShapeDtypeStruct((m, 512), BF16),
        jax.ShapeDtypeStruct((m // tm, 512, tm), BF16),
        jax.ShapeDtypeStruct((m, 512), BF16),
        jax.ShapeDtypeStruct((m // tm, 512, tm), BF16),
        jax.ShapeDtypeStruct((m, 512), BF16),
        jax.ShapeDtypeStruct((m // tm, 512, tm), BF16),
        jax.ShapeDtypeStruct((m, 512), BF16),
        jax.ShapeDtypeStruct((m, 1024), BF16),
        jax.ShapeDtypeStruct((m, 1024), BF16),
        jax.ShapeDtypeStruct((b, 8, s), F32),
        jax.ShapeDtypeStruct((m, 8), F32),
        jax.ShapeDtypeStruct((depth, b, 512, s), F32),
        jax.ShapeDtypeStruct((depth, b, 512, s), F32),
        jax.ShapeDtypeStruct((depth, b, 8, s), F32),
        jax.ShapeDtypeStruct((depth, b, 512, s), F32),
        jax.ShapeDtypeStruct((depth, b, s * 4, LANES), F32),
    ]
    t_spec = pl.BlockSpec((1, 512, tm), im3)
    r_spec = lambda width: pl.BlockSpec((tm, width), im2)
    st_spec = lambda rows: pl.BlockSpec((None, None, rows, tm), lambda bi, i: (layer, bi, 0, i))
    out_specs = [t_spec, r_spec(512), t_spec, r_spec(512), t_spec, r_spec(512), t_spec, r_spec(512),
                 r_spec(1024), r_spec(1024),
                 pl.BlockSpec((None, 8, tm), lambda bi, i: (bi, 0, i)), r_spec(8),
                 st_spec(512), st_spec(512), st_spec(8), st_spec(512),
                 pl.BlockSpec((None, None, tm * 4, LANES), lambda bi, i: (layer, bi, i, 0))]
    in_specs = ([pl.BlockSpec((None, tm, d), lambda bi, i: (bi, i, 0))]
                + [_layer_block(a, layer, single_buffer=True) for a in (g, w_t, wf_t, bf_col)]
                + [pl.BlockSpec(memory_space=pl.ANY)] * n_alias)
    return pl.pallas_call(
        functools.partial(_inproj_prompt_kernel, n_alias=n_alias),
        grid=(b, tps),
        in_specs=in_specs,
        out_specs=out_specs,
        out_shape=out_shape,
        scratch_shapes=[pltpu.VMEM((LANES, 1), F32)],
        input_output_aliases={5 + k: 12 + k for k in range(n_alias)},
        compiler_params=pltpu.CompilerParams(dimension_semantics=("arbitrary", "arbitrary"),
                                             vmem_limit_bytes=VMEM_LIMIT),
        name="inproj_prompt",
    )(x3d, g, w_t, wf_t, bf_col, *stacked)


def _inproj_sample_kernel(x_ref, g_ref, w_ref, wf_ref, bf_ref,
                          fq_ref, fk_ref, fv_ref, fz_ref, dq_ref, dk_ref, dv_ref, dz_ref, ga_ref, gb_ref, lf_ref):
    hb = _rms_rows(x_ref[...], g_ref).astype(BF16)

    def nat(r0, n):
        return lax.dot_general(hb, w_ref[r0:r0 + n, :], _NT, preferred_element_type=F32)

    fq_ref[...] = nat(_FQ, 512) * SCALE
    fk_ref[...] = nat(_FK, 512)
    fv_ref[...] = nat(_FV, 512)
    fz_ref[...] = nat(_FZ, 512)
    dq_ref[...] = nat(_DQ, 512) * SCALE
    dk_ref[...] = nat(_DK, 512)
    dv_ref[...] = nat(_DV, 512)
    dz_ref[...] = nat(_DZ, 512)
    ga_ref[...] = nat(_GA, 1024)
    gb_ref[...] = nat(_GB, 1024)
    ff = lax.dot_general(hb, wf_ref[...], _NT, preferred_element_type=F32) + bf_ref[...]
    lf_ref[...] = _log_sigmoid(ff)[:, :8]


def _inproj_sample(x2d, g, w_t, wf_t, bf_row, layer):
    m, d = x2d.shape
    full = lambda shape: pl.BlockSpec(shape, lambda i: (0,) * len(shape))
    widths = [512] * 8 + [1024, 1024, 8]
    return pl.pallas_call(
        _inproj_sample_kernel,
        grid=(1,),
        in_specs=[full((m, d))] + [_layer_block(a, layer) for a in (g, w_t, wf_t, bf_row)],
        out_specs=[full((m, w)) for w in widths],
        out_shape=[jax.ShapeDtypeStruct((m, w), F32) for w in widths],
        compiler_params=pltpu.CompilerParams(dimension_semantics=("arbitrary",), vmem_limit_bytes=VMEM_LIMIT),
        name="inproj_sample",
    )(x2d, g, w_t, wf_t, bf_row)


def _bias_lookup(rb_ref, bucket, head):
    out = jnp.zeros(bucket.shape, F32)
    for b in range(NUM_BUCKETS):
        out = jnp.where(bucket == b, rb_ref[b, head], out)
    return out


def _prompt_bias_kernel(rb_ref, out_ref, *, tq, tk):
    h = pl.program_id(0)
    key = lax.broadcasted_iota(jnp.int32, (tk, tq), 0)
    qry = lax.broadcasted_iota(jnp.int32, (tk, tq), 1)
    for t in range(3):
        bucket = _t5_bucket(qry - key + t * tk)
        out_ref[0, t] = _bias_lookup(rb_ref, bucket, h) * LOG2E


def _prompt_bias(rel_bias, tq, tk):
    return pl.pallas_call(
        functools.partial(_prompt_bias_kernel, tq=tq, tk=tk),
        grid=(4,),
        in_specs=[pl.BlockSpec(memory_space=pltpu.SMEM)],
        out_specs=pl.BlockSpec((1, 3, tk, tq), lambda h: (h, 0, 0, 0)),
        out_shape=jax.ShapeDtypeStruct((4, 3, tk, tq), F32),
        name="prompt_bias",
    )(rel_bias)


def _decode_bias_kernel(rb_ref, out_ref, self_ref, *, past_len, page):
    p = pl.program_id(0)
    pos = p * page + lax.broadcasted_iota(jnp.int32, (8, page), 1)
    row = lax.broadcasted_iota(jnp.int32, (8, page), 0)
    bucket = _t5_bucket(past_len - pos)
    bucket0 = _t5_bucket(jnp.zeros((8, page), jnp.int32))
    acc = jnp.zeros((8, page), F32)
    acc0 = jnp.zeros((8, page), F32)
    for hh in range(4):
        sel = (row // 2) == hh
        acc = jnp.where(sel, _bias_lookup(rb_ref, bucket, hh), acc)
        acc0 = jnp.where(sel, _bias_lookup(rb_ref, bucket0, hh), acc0)
    out_ref[0] = acc
    self_ref[...] = acc0


def _decode_bias(rel_bias, past_len, page):
    n_pages = past_len // page
    return pl.pallas_call(
        functools.partial(_decode_bias_kernel, past_len=past_len, page=page),
        grid=(n_pages,),
        in_specs=[pl.BlockSpec(memory_space=pltpu.SMEM)],
        out_specs=[pl.BlockSpec((1, 8, page), lambda p: (p, 0, 0)), pl.BlockSpec((8, page), lambda p: (0, 0))],
        out_shape=[jax.ShapeDtypeStruct((n_pages, 8, page), F32), jax.ShapeDtypeStruct((8, page), F32)],
        name="decode_bias",
    )(rel_bias)


def _attn_chunk(j, masked, qs, k_ref, vt_ref, sub_term, add_term, col_terms, m_sc, acc_sc, tk):
    start = pl.multiple_of(j * tk, tk)
    tq = qs[0].shape[1]
    if masked:
        key = lax.broadcasted_iota(jnp.int32, (tk, tq), 0)
        qry = lax.broadcasted_iota(jnp.int32, (tk, tq), 1)
        keep = key <= qry
    ts = []
    for n, qn in enumerate(qs):
        u = n // 2
        ks = k_ref[0, pl.ds(start, tk), u * LANES:(u + 1) * LANES]
        t = jnp.dot(ks, qn, preferred_element_type=F32)
        if sub_term is not None:
            t = t - sub_term(n, start)
        if add_term is not None:
            t = t + add_term(n)
        if masked:
            t = jnp.where(keep, t, NEG_INF)
        ts.append(t)
    ones = jnp.ones((SUM_ROWS, tk), BF16)
    for n, t in enumerate(ts):
        u = n // 2
        ct = col_terms[n]
        mcol = jnp.max(t, axis=0, keepdims=True)
        if ct is not None:
            mcol = mcol + ct
        m_old = m_sc[n]
        m_new = jnp.maximum(m_old, mcol)
        shift = m_new if ct is None else m_new - ct
        p = jnp.exp2(t - shift).astype(BF16)
        alpha = jnp.exp2(m_old - m_new)
        m_sc[n] = m_new
        vts = jnp.concatenate([vt_ref[0, j, u * LANES:(u + 1) * LANES, :], ones], axis=0)
        acc_sc[n] = alpha * acc_sc[n] + jnp.dot(vts, p, preferred_element_type=F32)


def _attn_init(m_sc, acc_sc):
    m_sc[...] = jnp.full(m_sc.shape, -jnp.inf, F32)
    acc_sc[...] = jnp.zeros_like(acc_sc)


def _attn_result(acc_sc, n):
    return acc_sc[n, :LANES, :] / acc_sc[n, LANES:LANES + 1, :]


def _split_q(qt_ref, n_units):
    qs = []
    for u in range(n_units):
        qt = qt_ref[0, 0, u * LANES:(u + 1) * LANES, :]
        row = lax.broadcasted_iota(jnp.int32, qt.shape, 0)
        zero = jnp.zeros_like(qt)
        qs += [jnp.where(row < HEAD_DIM, qt, zero), jnp.where(row >= HEAD_DIM, qt, zero)]
    return qs


def _fox_attn_kernel(qt_ref, k_ref, vt_ref, cq_ref, ck_ref, o_ref, ckb_sc, m_sc, acc_sc, *, tk, n_units):
    qi = pl.program_id(2)
    n_streams = 2 * n_units

    @pl.when(qi == 0)
    def _():
        ck = ck_ref[0, 0] * LOG2E
        for n in range(n_streams):
            ckb_sc[n] = jnp.broadcast_to(ck[:, n:n + 1], ckb_sc.shape[1:])

    qs = _split_q(qt_ref, n_units)
    cq = cq_ref[0, 0] * LOG2E
    col_terms = [cq[n:n + 1, :] for n in range(n_streams)]
    _attn_init(m_sc, acc_sc)

    def key_decay(n, start):
        ckn = ckb_sc[n, pl.ds(start, tk), :]
        return jnp.concatenate([ckn] * (qs[0].shape[1] // LANES), axis=1)

    def step(j, masked):
        _attn_chunk(j, masked, qs, k_ref, vt_ref, key_decay, None, col_terms, m_sc, acc_sc, tk)

    def body(j, carry):
        step(j, False)
        return carry

    lax.fori_loop(0, qi, body, 0)
    step(qi, True)
    row = lax.broadcasted_iota(jnp.int32, (LANES, qs[0].shape[1]), 0)
    for u in range(n_units):
        ot = jnp.where(row < HEAD_DIM, _attn_result(acc_sc, 2 * u), _attn_result(acc_sc, 2 * u + 1))
        o_ref[0, :, u * LANES:(u + 1) * LANES] = ot.T.astype(o_ref.dtype)


def _diff_attn_kernel(qt_ref, k_ref, vt_ref, bias_ref, lq1_ref, lk1_ref, lq2_ref, lk2_ref, g_ref,
                      o_ref, m_sc, acc_sc, *, tk, n_units, lam_init):
    qi = pl.program_id(2)
    n_streams = 2 * n_units
    qs = _split_q(qt_ref, n_units)
    _attn_init(m_sc, acc_sc)

    def step(j, tile, masked):
        _attn_chunk(j, masked, qs, k_ref, vt_ref, None, lambda n: bias_ref[n // 2, tile], [None] * n_streams,
                    m_sc, acc_sc, tk)

    def body(j, carry):
        step(j, 2, False)
        return carry

    lax.fori_loop(0, qi - 1, body, 0)

    @pl.when(qi >= 1)
    def _():
        step(qi - 1, 1, False)

    step(qi, 0, True)
    lam = _lam_value(lq1_ref[...], lk1_ref[...], lq2_ref[...], lk2_ref[...], lam_init)
    for u in range(n_units):
        ot = _attn_result(acc_sc, 2 * u) - lam * _attn_result(acc_sc, 2 * u + 1)
        y = ot * lax.rsqrt(jnp.mean(ot * ot, axis=0, keepdims=True) + RMS_EPS)
        o_ref[0, :, u * LANES:(u + 1) * LANES] = ((y * g_ref[...]) * (1.0 - lam_init)).T.astype(o_ref.dtype)


def _attn_common(b, s, tq, n_units):
    width = n_units * LANES
    nq = s // tq
    grid = (b, FOX_WIDTH // width, nq)
    qt_spec = pl.BlockSpec((1, 1, width, tq), lambda bi, g, qi: (bi, qi, g, 0))
    k_spec = pl.BlockSpec((1, s, width), lambda bi, g, qi: (bi, 0, g))
    vt_spec = pl.BlockSpec((1, nq, width, tq), lambda bi, g, qi: (bi, 0, g, 0))
    o_spec = pl.BlockSpec((1, tq, width), lambda bi, g, qi: (bi, qi, g))
    n_streams = 2 * n_units
    scratch = [pltpu.VMEM((n_streams, 1, tq), F32), pltpu.VMEM((n_streams, LANES + SUM_ROWS, tq), F32)]
    params = pltpu.CompilerParams(dimension_semantics=("arbitrary", "arbitrary", "arbitrary"),
                                  vmem_limit_bytes=VMEM_LIMIT)
    return grid, qt_spec, k_spec, vt_spec, o_spec, scratch, params


def _fox_attention(qt, kb, vt, cq, ck, *, tq, n_units):
    b, s, _ = kb.shape
    grid, qt_spec, k_spec, vt_spec, o_spec, scratch, params = _attn_common(b, s, tq, n_units)
    n_streams = 2 * n_units
    return pl.pallas_call(
        functools.partial(_fox_attn_kernel, tk=tq, n_units=n_units),
        grid=grid,
        in_specs=[qt_spec, k_spec, vt_spec,
                  pl.BlockSpec((1, 1, n_streams, tq), lambda bi, g, qi: (bi, g, 0, qi)),
                  pl.BlockSpec((1, 1, s, n_streams), lambda bi, g, qi: (bi, g, 0, 0))],
        out_specs=o_spec,
        out_shape=jax.ShapeDtypeStruct((b, s, FOX_WIDTH), BF16),
        scratch_shapes=[pltpu.VMEM((n_streams, s, LANES), F32)] + scratch,
        compiler_params=params,
        name="fox_attention",
    )(qt, kb, vt, cq, ck)


def _diff_attention(qt, kb, vt, bias, lq1, lk1, lq2, lk2, g_col, layer, *, tq, n_units, lam_init):
    b, s, _ = kb.shape
    grid, qt_spec, k_spec, vt_spec, o_spec, scratch, params = _attn_common(b, s, tq, n_units)
    return pl.pallas_call(
        functools.partial(_diff_attn_kernel, tk=tq, n_units=n_units, lam_init=lam_init),
        grid=grid,
        in_specs=[qt_spec, k_spec, vt_spec,
                  pl.BlockSpec((n_units, 3, tq, tq), lambda bi, g, qi: (g, 0, 0, 0))]
                 + [_layer_block(a, layer) for a in (lq1, lk1, lq2, lk2, g_col)],
        out_specs=o_spec,
        out_shape=jax.ShapeDtypeStruct((b, s, DIFF_WIDTH), BF16),
        scratch_shapes=scratch,
        compiler_params=params,
        name="diff_attention",
    )(qt, kb, vt, bias, lq1, lk1, lq2, lk2, g_col)


def _merge_kernel(of_ref, fz_ref, od_ref, dz_ref, ga_ref, gb_ref, x_ref, wbf_ref, wbd_ref, wo_ref, g_ref,
                  out_ref, *, final):
    fz = fz_ref[...].astype(F32)
    dz = dz_ref[...].astype(F32)
    a = (of_ref[...].astype(F32) * (fz * _sigmoid(fz))).astype(BF16)
    d = (od_ref[...].astype(F32) * (dz * _sigmoid(dz))).astype(BF16)
    y_fox = jnp.dot(a, wbf_ref[...], preferred_element_type=F32)
    y_diff = jnp.dot(d, wbd_ref[...], preferred_element_type=F32)
    m = _sigmoid(ga_ref[...].astype(F32)) * y_fox + _sigmoid(gb_ref[...].astype(F32)) * y_diff
    x = x_ref[...] + jnp.dot(m.astype(BF16), wo_ref[...], preferred_element_type=F32)
    if final:
        x = _rms_rows(x, g_ref)
    out_ref[...] = x


def _merge(o_f, fz, o_d, dz, ga, gb, x2d, w_bf, w_bd, w_o, g_final, layer, *, tm, final, name):
    m, d = x2d.shape
    row_spec = lambda width: pl.BlockSpec((tm, width), lambda i: (i, 0))
    return pl.pallas_call(
        functools.partial(_merge_kernel, final=final),
        grid=(m // tm,),
        in_specs=[row_spec(512), row_spec(512), row_spec(512), row_spec(512), row_spec(d), row_spec(d),
                  row_spec(d)] + [_layer_block(a, layer, single_buffer=True) for a in (w_bf, w_bd, w_o)]
                 + [pl.BlockSpec((1, d), lambda i: (0, 0), pipeline_mode=pl.Buffered(1))],
        out_specs=row_spec(d),
        out_shape=jax.ShapeDtypeStruct((m, d), F32),
        compiler_params=pltpu.CompilerParams(dimension_semantics=("arbitrary",), vmem_limit_bytes=VMEM_LIMIT),
        name=name,
    )(o_f, fz, o_d, dz, ga, gb, x2d, w_bf, w_bd, w_o, g_final)


def _decode_kernel(pt_ref, fk_hbm, lf_hbm, dk_hbm, fv_hbm, dv_hbm,
                   fq_ref, dq_ref, fkn_ref, fvn_ref, dkn_ref, dvn_ref, lfn_ref, bias_ref, bias0_ref,
                   lq1_ref, lk1_ref, lq2_ref, lk2_ref, g_ref,
                   of_ref, od_ref,
                   buf_a, buf_b, buf_lf, sem, sf_sc, sd_sc, w4_sc, *, base, n_slots, n_pages, page, lam_init):
    b = pl.program_id(0)
    n_chunks = n_pages // n_slots
    width = fq_ref.shape[-1]
    n_diff = width // LANES
    hv = lax.broadcasted_iota(jnp.int32, (8, width), 0)
    lane = lax.broadcasted_iota(jnp.int32, (8, width), 1)
    own = (lane // HEAD_DIM) == hv
    a_f = jnp.where(own, fq_ref[0], 0.0).astype(BF16)
    a_d = jnp.where(own, dq_ref[0], 0.0).astype(BF16)

    n_buf = buf_a.shape[0]
    ahead = n_buf - 1
    n_seq = pl.num_programs(0)

    def chunk_copies(kinds, seq, chunk):
        slot = chunk % n_buf
        out = []
        for i in range(n_slots):
            pg = base + pt_ref[seq, chunk * n_slots + i]
            for src, dst, col in kinds:
                out.append(pltpu.make_async_copy(src.at[pg], dst.at[slot, i], sem.at[slot, col]))
        return out

    def start(kinds, seq, chunk):
        for cp in chunk_copies(kinds, seq, chunk):
            cp.start()

    k_kinds = ((fk_hbm, buf_a, 0), (dk_hbm, buf_b, 1), (lf_hbm, buf_lf, 2))
    v_kinds = ((fv_hbm, buf_a, 0), (dv_hbm, buf_b, 1))

    def pipelined(kinds, body, init, next_kinds, next_seq, next_ok):
        def step(chunk, carry):
            nxt = chunk + ahead

            @pl.when(nxt < n_chunks)
            def _():
                start(kinds, b, nxt)

            @pl.when(jnp.logical_and(nxt >= n_chunks, next_ok))
            def _():
                start(next_kinds, next_seq, nxt - n_chunks)

            for cp in chunk_copies(kinds, b, chunk):
                cp.wait()
            return body(chunk, chunk % n_buf, carry)

        return lax.fori_loop(0, n_chunks, step, init)

    @pl.when(b == 0)
    def _():
        for c in range(ahead):
            start(k_kinds, 0, c)

    u = lax.broadcasted_iota(jnp.int32, (page, page), 0)
    s = lax.broadcasted_iota(jnp.int32, (page, page), 1)
    tri = jnp.where(u <= s, 1.0, 0.0).astype(BF16)

    def key_chunk(chunk, slot, run):
        lf = buf_lf[slot].reshape(n_slots * 8, page)
        h1, h2, h3 = _split3(lf)
        local = (jnp.dot(h1, tri, preferred_element_type=F32) + jnp.dot(h2, tri, preferred_element_type=F32)
                 + jnp.dot(h3, tri, preferred_element_type=F32))
        for i in range(n_slots):
            pg = chunk * n_slots + i
            pre = local[8 * i:8 * i + 8] + run
            run = jnp.broadcast_to(pre[:, page - 1:page], run.shape)
            qk = jnp.dot(a_f, buf_a[slot, i].astype(BF16), preferred_element_type=F32)
            sf_sc[pg] = qk - pre
            sd_sc[pg] = jnp.dot(a_d, buf_b[slot, i].astype(BF16), preferred_element_type=F32) + bias_ref[pg]
        return run

    run = pipelined(k_kinds, key_chunk, jnp.zeros((8, page), F32), v_kinds, b, True)

    def softmax_rows(logits, self_logit):
        mx = jnp.max(jnp.max(logits, axis=0), axis=1, keepdims=True)
        mx = jnp.maximum(mx, self_logit)
        p = jnp.exp(logits - mx[None])
        p_self = jnp.exp(self_logit - mx)
        den = jnp.sum(jnp.sum(p, axis=0), axis=1, keepdims=True) + p_self
        return p / den[None], p_self / den

    def self_logit(a, kn_ref):
        kn = kn_ref[0].astype(BF16).astype(F32)
        return jnp.sum(a.astype(F32) * kn, axis=1, keepdims=True)

    c_new = run[:, 0:1] + lfn_ref[0]
    w, wf_self = softmax_rows(sf_sc[...] + c_new[None], self_logit(a_f, fkn_ref) + (c_new - c_new))
    sf_sc[...] = w
    w, wd_self = softmax_rows(sd_sc[...], self_logit(a_d, dkn_ref) + bias0_ref[:, 0:1])
    lam = _lam_value(lq1_ref[...], lk1_ref[...], lq2_ref[...], lk2_ref[...], lam_init)
    rows = n_pages * 8
    w2 = w.reshape(rows, page)
    row = lax.broadcasted_iota(jnp.int32, (rows, page), 0)
    wc = jnp.where(row % 2 == 0, w2 - lam * pltpu.roll(w2, rows - 1, 0), 0.0)
    wd8 = jnp.broadcast_to(wd_self, (8, page))
    wd_self = jnp.where(row[:8] % 2 == 0, wd8 - lam * pltpu.roll(wd8, 7, 0), 0.0)[:, 0:1]
    pos = lax.broadcasted_iota(jnp.int32, (page, page * n_diff), 0)
    col = lax.broadcasted_iota(jnp.int32, (page, page * n_diff), 1)
    rep = jnp.where(col // n_diff == pos, 1.0, 0.0).astype(BF16)
    spread = jnp.dot(wc.astype(BF16), rep, preferred_element_type=F32)
    rowx = lax.broadcasted_iota(jnp.int32, spread.shape, 0)
    colx = lax.broadcasted_iota(jnp.int32, spread.shape, 1)
    w4_sc[...] = jnp.where((colx % n_diff) * 2 == rowx % 8, spread, 0.0).reshape(w4_sc.shape)

    def value_chunk(chunk, slot, carry):
        acc_f, acc_d = carry
        for i in range(n_slots):
            pg = chunk * n_slots + i
            acc_f = acc_f + lax.dot_general(sf_sc[pg].astype(BF16), buf_a[slot, i].astype(BF16), _NT,
                                            preferred_element_type=F32)
            acc_d = acc_d + jnp.dot(w4_sc[pg].astype(BF16), buf_b[slot, i].astype(BF16),
                                    preferred_element_type=F32)
        return acc_f, acc_d

    acc_f, acc_d = pipelined(v_kinds, value_chunk, (jnp.zeros((8, width), F32), jnp.zeros((8, LANES), F32)),
                             k_kinds, b + 1, b + 1 < n_seq)

    acc = acc_f + wf_self * fvn_ref[0]
    of_ref[0] = jnp.sum(jnp.where(own, acc, 0.0), axis=0, keepdims=True)
    dvn = dvn_ref[0]
    zero = jnp.zeros((1, LANES), F32)
    vn8 = jnp.concatenate([piece for h in range(n_diff)
                           for piece in (dvn[:, h * LANES:(h + 1) * LANES], zero)], axis=0)
    od8 = acc_d + wd_self * vn8
    y8 = od8 * lax.rsqrt(jnp.mean(od8 * od8, axis=1, keepdims=True) + RMS_EPS)
    y8 = (y8 * g_ref[...]) * (1.0 - lam_init)
    od_ref[0] = jnp.concatenate([y8[2 * h:2 * h + 1] for h in range(n_diff)], axis=1)


def _decode_attention(page_table, base, fk, lft, dk, fv, dv, fq, dq, fkn, fvn, dkn, dvn, lfn, bias, bias0,
                      lq1, lk1, lq2, lk2, g, layer, *, n_slots, lam_init):
    bs, n_pages = page_table.shape
    width, page = fk.shape[1], fk.shape[2]
    n_buf = 4
    assert dv.shape[1:] == fk.shape[1:] and n_pages % (n_slots * n_buf) == 0

    row_spec = pl.BlockSpec((1, 1, width), lambda b, pt: (b, 0, 0))
    full = lambda a: pl.BlockSpec(a.shape, lambda b, pt: (0,) * a.ndim)
    in_specs = ([pl.BlockSpec(memory_space=pl.ANY)] * 5
                + [row_spec] * 6
                + [pl.BlockSpec((1, 8, 1), lambda b, pt: (b, 0, 0)), full(bias), full(bias0)]
                + [_layer_block(a, layer) for a in (lq1, lk1, lq2, lk2, g)])
    scratch = [pltpu.VMEM((n_buf, n_slots, width, page), F32), pltpu.VMEM((n_buf, n_slots, width, page), F32),
               pltpu.VMEM((n_buf, n_slots, 8, page), F32), pltpu.SemaphoreType.DMA((n_buf, 3)),
               pltpu.VMEM((n_pages, 8, page), F32), pltpu.VMEM((n_pages, 8, page), F32),
               pltpu.VMEM((n_pages, 8, dv.shape[1]), F32)]
    grid_spec = pltpu.PrefetchScalarGridSpec(
        num_scalar_prefetch=1, grid=(bs,), in_specs=in_specs,
        out_specs=[row_spec, row_spec], scratch_shapes=scratch)
    return pl.pallas_call(
        functools.partial(_decode_kernel, base=base, n_slots=n_slots, n_pages=n_pages, page=page,
                          lam_init=lam_init),
        grid_spec=grid_spec,
        out_shape=[jax.ShapeDtypeStruct((bs, 1, width), F32), jax.ShapeDtypeStruct((bs, 1, width), F32)],
        compiler_params=pltpu.CompilerParams(dimension_semantics=("arbitrary",), vmem_limit_bytes=VMEM_LIMIT),
        name="decode_attention",
    )(page_table, fk, lft, dk, fv, dv, fq, dq, fkn, fvn, dkn, dvn, lfn, bias, bias0, lq1, lk1, lq2, lk2, g)


def kernel(x_prompt, x_sample, cache_fox_k, cache_fox_v, cache_fox_logf, cache_diff_k, cache_diff_v, page_table,
           norm_g, w_in, b_forget, lambda_q1, lambda_k1, lambda_q2, lambda_k2, diff_subln_g, w_branch_fox,
           w_branch_diff, w_out, rel_bias, final_norm_g):
    depth = w_in.shape[0]
    b, s, d = x_prompt.shape
    bs = x_sample.shape[0]
    n_pool, page = cache_fox_k.shape[1], cache_fox_k.shape[2]
    n_pages = page_table.shape[1]
    past_len = n_pages * page
    n_fox = FOX_WIDTH // HEAD_DIM
    n_diff = DIFF_WIDTH // (2 * HEAD_DIM)
    tq = 256
    n_units = 4
    n_slots = 4

    w_t = jnp.swapaxes(w_in, 1, 2)
    w_main = jnp.concatenate([w_t[:, :3 * FOX_WIDTH], w_t[:, 3 * FOX_WIDTH + n_fox:]], axis=1).astype(BF16)
    w_f = jnp.pad(w_t[:, 3 * FOX_WIDTH:3 * FOX_WIDTH + n_fox], ((0, 0), (0, LANES - n_fox), (0, 0))).astype(BF16)
    b_f = jnp.pad(b_forget, ((0, 0), (0, LANES - n_fox)))
    w_bf = w_branch_fox.astype(BF16)
    w_bd = w_branch_diff.astype(BF16)
    w_o = w_out.astype(BF16)

    fk_c = cache_fox_k.transpose(0, 1, 3, 4, 2).reshape(depth * n_pool, FOX_WIDTH, page)
    fv_c = cache_fox_v.transpose(0, 1, 3, 4, 2).reshape(depth * n_pool, FOX_WIDTH, page)
    dk_c = cache_diff_k.transpose(0, 1, 3, 4, 5, 2).reshape(depth * n_pool, DIFF_WIDTH, page)
    dv_c = cache_diff_v.reshape(depth * n_pool, page * n_diff, 2 * HEAD_DIM)
    lf_c = cache_fox_logf.transpose(0, 1, 3, 2).reshape(depth * n_pool, n_fox, page)

    bias_p = _prompt_bias(rel_bias, tq, tq)
    bias_s, bias_s0 = _decode_bias(rel_bias, past_len, page)

    xp = x_prompt
    xs = x_sample.reshape(bs, d)
    stacked = (jnp.zeros((depth, b, FOX_WIDTH, s), F32), jnp.zeros((depth, b, FOX_WIDTH, s), F32),
               jnp.zeros((depth, b, n_fox, s), F32), jnp.zeros((depth, b, DIFF_WIDTH, s), F32),
               jnp.zeros((depth, b, s * n_diff, 2 * HEAD_DIM), F32))
    outs_s = [[] for _ in range(5)]
    g = norm_g[:, None, :]
    lvec = [a[:, None, :] for a in (lambda_q1, lambda_k1, lambda_q2, lambda_k2)]
    for l in range(depth):
        lam_init = _lambda_init(l)
        final = l == depth - 1

        (fqt, fkb, fvt, fz, dqt, dkb, dvt, dz, ga, gb, ct, c, *stacked) = _inproj_prompt(
            xp, g, w_main, w_f, b_f[:, :, None], tuple(stacked), l, depth, tm=tq)
        chunks = lambda a: a.reshape(b, s // tq, a.shape[1], tq)
        rows = lambda a: a.reshape(b, s, -1)
        n_streams = 2 * n_units
        cq = ct.reshape(b, n_fox // n_streams, n_streams, s)
        ck = c.reshape(b, s, n_fox // n_streams, n_streams).transpose(0, 2, 1, 3)
        o_f = _fox_attention(chunks(fqt), rows(fkb), chunks(fvt), cq, ck, tq=tq, n_units=n_units)
        o_d = _diff_attention(chunks(dqt), rows(dkb), chunks(dvt), bias_p, *lvec, diff_subln_g[:, :, None], l,
                              tq=tq, n_units=n_units, lam_init=lam_init)
        xp = _merge(o_f.reshape(b * s, -1), fz, o_d.reshape(b * s, -1), dz, ga, gb, xp.reshape(b * s, d),
                    w_bf, w_bd, w_o, final_norm_g[None], l, tm=tq, final=final,
                    name="merge_prompt").reshape(b, s, d)

        sq, sk, sv, sz, sdq, sdk, sdv, sdz, sga, sgb, slf = _inproj_sample(xs, g, w_main, w_f, b_f[:, None, :], l)
        row3 = lambda a: a.reshape(bs, 1, -1)
        o_fs, o_ds = _decode_attention(
            page_table, l * n_pool, fk_c, lf_c, dk_c, fv_c, dv_c,
            row3(sq), row3(sdq), row3(sk), row3(sv), row3(sdk), row3(sdv), slf.reshape(bs, n_fox, 1),
            bias_s, bias_s0, *lvec, diff_subln_g[:, None, :], l, n_slots=n_slots, lam_init=lam_init)
        xs = _merge(o_fs.reshape(bs, -1), sz, o_ds.reshape(bs, -1), sdz, sga, sgb, xs, w_bf, w_bd, w_o,
                    final_norm_g[None], l, tm=bs, final=final, name="merge_sample")
        for acc, a in zip(outs_s, (sk, sv, slf, sdk, sdv)):
            acc.append(a)

    pk, pv, plf, pdk, pdv = stacked
    sk_l, sv_l, slf_l, sdk_l, sdv_l = [jnp.stack(a) for a in outs_s]
    return (xp, xs.reshape(bs, 1, d),
            pk.reshape(depth, b, n_fox, HEAD_DIM, s).transpose(0, 1, 4, 2, 3),
            pv.reshape(depth, b, n_fox, HEAD_DIM, s).transpose(0, 1, 4, 2, 3),
            plf.transpose(0, 1, 3, 2),
            pdk.reshape(depth, b, n_diff, 2, HEAD_DIM, s).transpose(0, 1, 5, 2, 3, 4),
            pdv.reshape(depth, b, s, n_diff, 2 * HEAD_DIM),
            sk_l.reshape(depth, bs, 1, n_fox, HEAD_DIM), sv_l.reshape(depth, bs, 1, n_fox, HEAD_DIM),
            slf_l.reshape(depth, bs, 1, n_fox),
            sdk_l.reshape(depth, bs, 1, n_diff, 2, HEAD_DIM), sdv_l.reshape(depth, bs, 1, n_diff, 2 * HEAD_DIM))
```

```python
import functools
import math

import jax
import jax.numpy as jnp
from jax import lax
from jax.experimental import pallas as pl
from jax.experimental.pallas import tpu as pltpu

HEAD_DIM = 64
LANES = 128
FOX_WIDTH = 512
DIFF_WIDTH = 512
NUM_BUCKETS = 32
MAX_DISTANCE = 128
RMS_EPS = 1e-6
NEG_INF = -1e30
SCALE = HEAD_DIM ** -0.5
LOG2E = math.log2(math.e)
SUM_ROWS = 16
VMEM_LIMIT = 56 * 1024 * 1024

F32 = jnp.float32
BF16 = jnp.bfloat16

_NT = (((1,), (1,)), ((), ()))


def _lambda_init(layer):
    return 0.8 - 0.6 * math.exp(-0.3 * layer)


def _sigmoid(x):
    return 1.0 / (1.0 + jnp.exp(-x))


def _log_sigmoid(x):
    return jnp.minimum(x, 0.0) - jnp.log1p(jnp.exp(-jnp.abs(x)))


def _split3(x):
    h1 = x.astype(BF16)
    r1 = x - h1.astype(F32)
    h2 = r1.astype(BF16)
    r2 = r1 - h2.astype(F32)
    return h1, h2, r2.astype(BF16)


def _prefix_sums(x, n):
    u = lax.broadcasted_iota(jnp.int32, (n, n), 0)
    s = lax.broadcasted_iota(jnp.int32, (n, n), 1)
    tri = jnp.where(u <= s, 1.0, 0.0).astype(BF16)
    h1, h2, h3 = _split3(x)
    return (jnp.dot(h1, tri, preferred_element_type=F32) + jnp.dot(h2, tri, preferred_element_type=F32)
            + jnp.dot(h3, tri, preferred_element_type=F32))


def _t5_bucket(rel):
    n = jnp.maximum(rel, 0)
    max_exact = NUM_BUCKETS // 2
    nf = jnp.maximum(n, 1).astype(F32)
    large = max_exact + (jnp.log(nf / max_exact) / math.log(MAX_DISTANCE / max_exact)
                         * (NUM_BUCKETS - max_exact)).astype(jnp.int32)
    large = jnp.minimum(large, NUM_BUCKETS - 1)
    return jnp.where(n < max_exact, n, large)


def _lam_value(lq1, lk1, lq2, lk2, lam_init):
    a = jnp.sum(lq1 * lk1, axis=1, keepdims=True)
    b = jnp.sum(lq2 * lk2, axis=1, keepdims=True)
    return jnp.exp(a) - jnp.exp(b) + lam_init


def _layer_block(a, layer, single_buffer=False):
    idx = (layer,) + (0,) * (a.ndim - 1)
    kwargs = dict(pipeline_mode=pl.Buffered(1)) if single_buffer else {}
    return pl.BlockSpec((None,) + a.shape[1:], lambda *_: idx, **kwargs)


def _rms_rows(x, g_ref):
    y = x * lax.rsqrt(jnp.mean(x * x, axis=-1, keepdims=True) + RMS_EPS)
    return y * g_ref[...]


_FQ, _FK, _FV, _FZ, _DQ, _DK, _DV, _DZ, _GA, _GB = (0, 512, 1024, 1536, 2048, 2560, 3072, 3584, 4096, 5120)


def _inproj_prompt_kernel(*refs, n_alias):
    x_ref, g_ref, w_ref, wf_ref, bf_ref = refs[:5]
    (fqt_ref, fk_ref, fvt_ref, fz_ref, dqt_ref, dk_ref, dvt_ref, dz_ref, ga_ref, gb_ref, ct_ref, c_ref,
     pk_ref, pv_ref, plf_ref, pdk_ref, pdv_ref, carry_ref) = refs[5 + n_alias:]
    hb = _rms_rows(x_ref[...], g_ref).astype(BF16)
    tm = hb.shape[0]

    def tr(r0, n):
        return lax.dot_general(w_ref[r0:r0 + n, :], hb, _NT, preferred_element_type=F32)

    def nat(r0, n):
        return lax.dot_general(hb, w_ref[r0:r0 + n, :], _NT, preferred_element_type=F32)

    fqt_ref[0] = (tr(_FQ, 512) * (SCALE * LOG2E)).astype(BF16)
    kt = tr(_FK, 512)
    pk_ref[...] = kt
    fk_ref[...] = kt.T.astype(BF16)
    vt = tr(_FV, 512)
    pv_ref[...] = vt
    fvt_ref[0] = vt.astype(BF16)
    fz_ref[...] = nat(_FZ, 512).astype(BF16)
    dqt_ref[0] = (tr(_DQ, 512) * (SCALE * LOG2E)).astype(BF16)
    kt = tr(_DK, 512)
    pdk_ref[...] = kt
    dk_ref[...] = kt.T.astype(BF16)
    v = nat(_DV, 512)
    n_dv = v.shape[1] // LANES
    for h in range(n_dv):
        pdv_ref[pl.ds(h, tm, stride=n_dv), :] = v[:, h * LANES:(h + 1) * LANES]
    dvt_ref[0] = v.T.astype(BF16)
    dz_ref[...] = nat(_DZ, 512).astype(BF16)
    ga_ref[...] = nat(_GA, 1024).astype(BF16)
    gb_ref[...] = nat(_GB, 1024).astype(BF16)

    fft = lax.dot_general(wf_ref[...], hb, _NT, preferred_element_type=F32) + bf_ref[...]
    lft = _log_sigmoid(fft)
    plf_ref[...] = lft[:8]

    @pl.when(pl.program_id(1) == 0)
    def _():
        carry_ref[...] = jnp.zeros_like(carry_ref)

    ct = _prefix_sums(lft, tm) + carry_ref[...]
    carry_ref[...] = ct[:, tm - 1:tm]
    ct_ref[...] = ct[:8]
    c_ref[...] = ct.T[:, :8]


def _inproj_prompt(x3d, g, w_t, wf_t, bf_col, stacked, layer, depth, *, tm):
    b, s, d = x3d.shape
    tps = s // tm
    n_alias = len(stacked)
    im2 = lambda bi, i: (bi * tps + i, 0)
    im3 = lambda bi, i: (bi * tps + i, 0, 0)
    m = b * s
    out_shape = [
        jax.ShapeDtypeStruct((m // tm, 512, tm), BF16),
        jax.ShapeDtypeStruct((m, 512), BF16),
        jax.ShapeDtypeStruct((m // tm, 512, tm), BF16),
        jax.ShapeDtypeStruct((m, 512), BF16),
        jax.ShapeDtypeStruct((m // tm, 512, tm), BF16),
        jax.ShapeDtypeStruct((m, 512), BF16),
        jax.ShapeDtypeStruct((m // tm, 512, tm), BF16),
        jax.ShapeDtypeStruct((m, 512), BF16),
        jax.ShapeDtypeStruct((m, 1024), BF16),
        jax.ShapeDtypeStruct((m, 1024), BF16),
        jax.ShapeDtypeStruct((b, 8, s), F32),
        jax.ShapeDtypeStruct((m, 8), F32),
        jax.ShapeDtypeStruct((depth, b, 512, s), F32),
        jax.ShapeDtypeStruct((depth, b, 512, s), F32),
        jax.ShapeDtypeStruct((depth, b, 8, s), F32),
        jax.ShapeDtypeStruct((depth, b, 512, s), F32),
        jax.ShapeDtypeStruct((depth, b, s * 4, LANES), F32),
    ]
    t_spec = pl.BlockSpec((1, 512, tm), im3)
    r_spec = lambda width: pl.BlockSpec((tm, width), im2)
    st_spec = lambda rows: pl.BlockSpec((None, None, rows, tm), lambda bi, i: (layer, bi, 0, i))
    out_specs = [t_spec, r_spec(512), t_spec, r_spec(512), t_spec, r_spec(512), t_spec, r_spec(512),
                 r_spec(1024), r_spec(1024),
                 pl.BlockSpec((None, 8, tm), lambda bi, i: (bi, 0, i)), r_spec(8),
                 st_spec(512), st_spec(512), st_spec(8), st_spec(512),
                 pl.BlockSpec((None, None, tm * 4, LANES), lambda bi, i: (layer, bi, i, 0))]
    in_specs = ([pl.BlockSpec((None, tm, d), lambda bi, i: (bi, i, 0))]
                + [_layer_block(a, layer, single_buffer=True) for a in (g, w_t, wf_t, bf_col)]
                + [pl.BlockSpec(memory_space=pl.ANY)] * n_alias)
    return pl.pallas_call(
        functools.partial(_inproj_prompt_kernel, n_alias=n_alias),
        grid=(b, tps),
        in_specs=in_specs,
        out_specs=out_specs,
        out_shape=out_shape,
        scratch_shapes=[pltpu.VMEM((LANES, 1), F32)],
        input_output_aliases={5 + k: 12 + k for k in range(n_alias)},
        compiler_params=pltpu.CompilerParams(dimension_semantics=("arbitrary", "arbitrary"),
                                             vmem_limit_bytes=VMEM_LIMIT),
        name="inproj_prompt",
    )(x3d, g, w_t, wf_t, bf_col, *stacked)


def _inproj_sample_kernel(x_ref, g_ref, w_ref, wf_ref, bf_ref,
                          fq_ref, fk_ref, fv_ref, fz_ref, dq_ref, dk_ref, dv_ref, dz_ref, ga_ref, gb_ref, lf_ref):
    hb = _rms_rows(x_ref[...], g_ref).astype(BF16)

    def nat(r0, n):
        return lax.dot_general(hb, w_ref[r0:r0 + n, :], _NT, preferred_element_type=F32)

    fq_ref[...] = nat(_FQ, 512) * SCALE
    fk_ref[...] = nat(_FK, 512)
    fv_ref[...] = nat(_FV, 512)
    fz_ref[...] = nat(_FZ, 512)
    dq_ref[...] = nat(_DQ, 512) * SCALE
    dk_ref[...] = nat(_DK, 512)
    dv_ref[...] = nat(_DV, 512)
    dz_ref[...] = nat(_DZ, 512)
    ga_ref[...] = nat(_GA, 1024)
    gb_ref[...] = nat(_GB, 1024)
    ff = lax.dot_general(hb, wf_ref[...], _NT, preferred_element_type=F32) + bf_ref[...]
    lf_ref[...] = _log_sigmoid(ff)[:, :8]


def _inproj_sample(x2d, g, w_t, wf_t, bf_row, layer):
    m, d = x2d.shape
    full = lambda shape: pl.BlockSpec(shape, lambda i: (0,) * len(shape))
    widths = [512] * 8 + [1024, 1024, 8]
    return pl.pallas_call(
        _inproj_sample_kernel,
        grid=(1,),
        in_specs=[full((m, d))] + [_layer_block(a, layer) for a in (g, w_t, wf_t, bf_row)],
        out_specs=[full((m, w)) for w in widths],
        out_shape=[jax.ShapeDtypeStruct((m, w), F32) for w in widths],
        compiler_params=pltpu.CompilerParams(dimension_semantics=("arbitrary",), vmem_limit_bytes=VMEM_LIMIT),
        name="inproj_sample",
    )(x2d, g, w_t, wf_t, bf_row)


def _bias_lookup(rb_ref, bucket, head):
    out = jnp.zeros(bucket.shape, F32)
    for b in range(NUM_BUCKETS):
        out = jnp.where(bucket == b, rb_ref[b, head], out)
    return out


def _prompt_bias_kernel(rb_ref, out_ref, *, tq, tk):
    h = pl.program_id(0)
    key = lax.broadcasted_iota(jnp.int32, (tk, tq), 0)
    qry = lax.broadcasted_iota(jnp.int32, (tk, tq), 1)
    for t in range(3):
        bucket = _t5_bucket(qry - key + t * tk)
        out_ref[0, t] = _bias_lookup(rb_ref, bucket, h) * LOG2E


def _prompt_bias(rel_bias, tq, tk):
    return pl.pallas_call(
        functools.partial(_prompt_bias_kernel, tq=tq, tk=tk),
        grid=(4,),
        in_specs=[pl.BlockSpec(memory_space=pltpu.SMEM)],
        out_specs=pl.BlockSpec((1, 3, tk, tq), lambda h: (h, 0, 0, 0)),
        out_shape=jax.ShapeDtypeStruct((4, 3, tk, tq), F32),
        name="prompt_bias",
    )(rel_bias)


def _decode_bias_kernel(rb_ref, out_ref, self_ref, *, past_len, page):
    p = pl.program_id(0)
    pos = p * page + lax.broadcasted_iota(jnp.int32, (8, page), 1)
    row = lax.broadcasted_iota(jnp.int32, (8, page), 0)
    bucket = _t5_bucket(past_len - pos)
    bucket0 = _t5_bucket(jnp.zeros((8, page), jnp.int32))
    acc = jnp.zeros((8, page), F32)
    acc0 = jnp.zeros((8, page), F32)
    for hh in range(4):
        sel = (row // 2) == hh
        acc = jnp.where(sel, _bias_lookup(rb_ref, bucket, hh), acc)
        acc0 = jnp.where(sel, _bias_lookup(rb_ref, bucket0, hh), acc0)
    out_ref[0] = acc
    self_ref[...] = acc0


def _decode_bias(rel_bias, past_len, page):
    n_pages = past_len // page
    return pl.pallas_call(
        functools.partial(_decode_bias_kernel, past_len=past_len, page=page),
        grid=(n_pages,),
        in_specs=[pl.BlockSpec(memory_space=pltpu.SMEM)],
        out_specs=[pl.BlockSpec((1, 8, page), lambda p: (p, 0, 0)), pl.BlockSpec((8, page), lambda p: (0, 0))],
        out_shape=[jax.ShapeDtypeStruct((n_pages, 8, page), F32), jax.ShapeDtypeStruct((8, page), F32)],
        name="decode_bias",
    )(rel_bias)


def _attn_chunk(j, n_sub, masked, qs, k_ref, vt_ref, sub_term, add_term, col_terms, m_sc, acc_sc, tk):
    start = pl.multiple_of(j * tk, tk)
    size = n_sub * tk
    tq = qs[0].shape[1]
    if masked:
        key = lax.broadcasted_iota(jnp.int32, (size, tq), 0)
        qry = lax.broadcasted_iota(jnp.int32, (size, tq), 1)
        keep = key <= qry
    ts = []
    for n, qn in enumerate(qs):
        u = n // 2
        ks = k_ref[0, pl.ds(start, size), u * LANES:(u + 1) * LANES]
        t = jnp.dot(ks, qn, preferred_element_type=F32)
        if sub_term is not None:
            t = t - sub_term(n, start, size)
        if add_term is not None:
            t = t + jnp.concatenate([add_term(n)] * n_sub, axis=0)
        if masked:
            t = jnp.where(keep, t, NEG_INF)
        ts.append(t)
    ones = jnp.ones((SUM_ROWS, size), BF16)
    for n, t in enumerate(ts):
        u = n // 2
        ct = col_terms[n]
        mcol = jnp.max(t, axis=0, keepdims=True)
        if ct is not None:
            mcol = mcol + ct
        m_old = m_sc[n]
        m_new = jnp.maximum(m_old, mcol)
        shift = m_new if ct is None else m_new - ct
        p = jnp.exp2(t - shift).astype(BF16)
        alpha = jnp.exp2(m_old - m_new)
        m_sc[n] = m_new
        vts = jnp.concatenate([vt_ref[0, j + i, u * LANES:(u + 1) * LANES, :] for i in range(n_sub)], axis=1)
        vts = jnp.concatenate([vts, ones], axis=0)
        acc_sc[n] = alpha * acc_sc[n] + jnp.dot(vts, p, preferred_element_type=F32)


def _for_key_chunks(n_chunks, update):
    def pair(jp, carry):
        update(2 * jp, 2)
        return carry

    lax.fori_loop(0, n_chunks // 2, pair, 0)

    @pl.when(n_chunks % 2 == 1)
    def _():
        update(n_chunks - 1, 1)


def _attn_init(m_sc, acc_sc):
    m_sc[...] = jnp.full(m_sc.shape, -jnp.inf, F32)
    acc_sc[...] = jnp.zeros_like(acc_sc)


def _attn_result(acc_sc, n):
    return acc_sc[n, :LANES, :] / acc_sc[n, LANES:LANES + 1, :]


def _split_q(qt_ref, n_units):
    qs = []
    for u in range(n_units):
        qt = qt_ref[0, 0, u * LANES:(u + 1) * LANES, :]
        row = lax.broadcasted_iota(jnp.int32, qt.shape, 0)
        zero = jnp.zeros_like(qt)
        qs += [jnp.where(row < HEAD_DIM, qt, zero), jnp.where(row >= HEAD_DIM, qt, zero)]
    return qs


def _fox_attn_kernel(qt_ref, k_ref, vt_ref, cq_ref, ck_ref, o_ref, ckb_sc, m_sc, acc_sc, *, tk, n_units):
    qi = pl.program_id(2)
    n_streams = 2 * n_units

    @pl.when(qi == 0)
    def _():
        ck = ck_ref[0, 0] * LOG2E
        for n in range(n_streams):
            ckb_sc[n] = jnp.broadcast_to(ck[:, n:n + 1], ckb_sc.shape[1:])

    qs = _split_q(qt_ref, n_units)
    cq = cq_ref[0, 0] * LOG2E
    col_terms = [cq[n:n + 1, :] for n in range(n_streams)]
    _attn_init(m_sc, acc_sc)

    def key_decay(n, start, size):
        ckn = ckb_sc[n, pl.ds(start, size), :]
        return jnp.concatenate([ckn] * (qs[0].shape[1] // LANES), axis=1)

    def step(j, n_sub, masked):
        _attn_chunk(j, n_sub, masked, qs, k_ref, vt_ref, key_decay, None, col_terms, m_sc, acc_sc, tk)

    _for_key_chunks(qi, lambda j, n_sub: step(j, n_sub, False))
    step(qi, 1, True)
    row = lax.broadcasted_iota(jnp.int32, (LANES, qs[0].shape[1]), 0)
    for u in range(n_units):
        ot = jnp.where(row < HEAD_DIM, _attn_result(acc_sc, 2 * u), _attn_result(acc_sc, 2 * u + 1))
        o_ref[0, :, u * LANES:(u + 1) * LANES] = ot.T.astype(o_ref.dtype)


def _diff_attn_kernel(qt_ref, k_ref, vt_ref, bias_ref, lq1_ref, lk1_ref, lq2_ref, lk2_ref, g_ref,
                      o_ref, m_sc, acc_sc, *, tk, n_units, lam_init):
    qi = pl.program_id(2)
    n_streams = 2 * n_units
    qs = _split_q(qt_ref, n_units)
    _attn_init(m_sc, acc_sc)

    def step(j, n_sub, tile, masked):
        _attn_chunk(j, n_sub, masked, qs, k_ref, vt_ref, None, lambda n: bias_ref[n // 2, tile],
                    [None] * n_streams, m_sc, acc_sc, tk)

    _for_key_chunks(jnp.maximum(qi - 1, 0), lambda j, n_sub: step(j, n_sub, 2, False))

    @pl.when(qi >= 1)
    def _():
        step(qi - 1, 1, 1, False)

    step(qi, 1, 0, True)
    lam = _lam_value(lq1_ref[...], lk1_ref[...], lq2_ref[...], lk2_ref[...], lam_init)
    for u in range(n_units):
        ot = _attn_result(acc_sc, 2 * u) - lam * _attn_result(acc_sc, 2 * u + 1)
        y = ot * lax.rsqrt(jnp.mean(ot * ot, axis=0, keepdims=True) + RMS_EPS)
        o_ref[0, :, u * LANES:(u + 1) * LANES] = ((y * g_ref[...]) * (1.0 - lam_init)).T.astype(o_ref.dtype)


def _attn_common(b, s, tq, n_units):
    width = n_units * LANES
    nq = s // tq
    grid = (b, FOX_WIDTH // width, nq)
    qt_spec = pl.BlockSpec((1, 1, width, tq), lambda bi, g, qi: (bi, qi, g, 0))
    k_spec = pl.BlockSpec((1, s, width), lambda bi, g, qi: (bi, 0, g))
    vt_spec = pl.BlockSpec((1, nq, width, tq), lambda bi, g, qi: (bi, 0, g, 0))
    o_spec = pl.BlockSpec((1, tq, width), lambda bi, g, qi: (bi, qi, g))
    n_streams = 2 * n_units
    scratch = [pltpu.VMEM((n_streams, 1, tq), F32), pltpu.VMEM((n_streams, LANES + SUM_ROWS, tq), F32)]
    params = pltpu.CompilerParams(dimension_semantics=("arbitrary", "arbitrary", "arbitrary"),
                                  vmem_limit_bytes=VMEM_LIMIT)
    return grid, qt_spec, k_spec, vt_spec, o_spec, scratch, params


def _fox_attention(qt, kb, vt, cq, ck, *, tq, n_units):
    b, s, _ = kb.shape
    grid, qt_spec, k_spec, vt_spec, o_spec, scratch, params = _attn_common(b, s, tq, n_units)
    n_streams = 2 * n_units
    return pl.pallas_call(
        functools.partial(_fox_attn_kernel, tk=tq, n_units=n_units),
        grid=grid,
        in_specs=[qt_spec, k_spec, vt_spec,
                  pl.BlockSpec((1, 1, n_streams, tq), lambda bi, g, qi: (bi, g, 0, qi)),
                  pl.BlockSpec((1, 1, s, n_streams), lambda bi, g, qi: (bi, g, 0, 0))],
        out_specs=o_spec,
        out_shape=jax.ShapeDtypeStruct((b, s, FOX_WIDTH), BF16),
        scratch_shapes=[pltpu.VMEM((n_streams, s, LANES), F32)] + scratch,
        compiler_params=params,
        name="fox_attention",
    )(qt, kb, vt, cq, ck)


def _diff_attention(qt, kb, vt, bias, lq1, lk1, lq2, lk2, g_col, layer, *, tq, n_units, lam_init):
    b, s, _ = kb.shape
    grid, qt_spec, k_spec, vt_spec, o_spec, scratch, params = _attn_common(b, s, tq, n_units)
    return pl.pallas_call(
        functools.partial(_diff_attn_kernel, tk=tq, n_units=n_units, lam_init=lam_init),
        grid=grid,
        in_specs=[qt_spec, k_spec, vt_spec,
                  pl.BlockSpec((n_units, 3, tq, tq), lambda bi, g, qi: (g, 0, 0, 0))]
                 + [_layer_block(a, layer) for a in (lq1, lk1, lq2, lk2, g_col)],
        out_specs=o_spec,
        out_shape=jax.ShapeDtypeStruct((b, s, DIFF_WIDTH), BF16),
        scratch_shapes=scratch,
        compiler_params=params,
        name="diff_attention",
    )(qt, kb, vt, bias, lq1, lk1, lq2, lk2, g_col)


def _merge_kernel(of_ref, fz_ref, od_ref, dz_ref, ga_ref, gb_ref, x_ref, wbf_ref, wbd_ref, wo_ref, g_ref,
                  out_ref, *, final):
    fz = fz_ref[...].astype(F32)
    dz = dz_ref[...].astype(F32)
    a = (of_ref[...].astype(F32) * (fz * _sigmoid(fz))).astype(BF16)
    d = (od_ref[...].astype(F32) * (dz * _sigmoid(dz))).astype(BF16)
    y_fox = jnp.dot(a, wbf_ref[...], preferred_element_type=F32)
    y_diff = jnp.dot(d, wbd_ref[...], preferred_element_type=F32)
    m = _sigmoid(ga_ref[...].astype(F32)) * y_fox + _sigmoid(gb_ref[...].astype(F32)) * y_diff
    x = x_ref[...] + jnp.dot(m.astype(BF16), wo_ref[...], preferred_element_type=F32)
    if final:
        x = _rms_rows(x, g_ref)
    out_ref[...] = x


def _merge(o_f, fz, o_d, dz, ga, gb, x2d, w_bf, w_bd, w_o, g_final, layer, *, tm, final, name):
    m, d = x2d.shape
    row_spec = lambda width: pl.BlockSpec((tm, width), lambda i: (i, 0))
    return pl.pallas_call(
        functools.partial(_merge_kernel, final=final),
        grid=(m // tm,),
        in_specs=[row_spec(512), row_spec(512), row_spec(512), row_spec(512), row_spec(d), row_spec(d),
                  row_spec(d)] + [_layer_block(a, layer, single_buffer=True) for a in (w_bf, w_bd, w_o)]
                 + [pl.BlockSpec((1, d), lambda i: (0, 0), pipeline_mode=pl.Buffered(1))],
        out_specs=row_spec(d),
        out_shape=jax.ShapeDtypeStruct((m, d), F32),
        compiler_params=pltpu.CompilerParams(dimension_semantics=("arbitrary",), vmem_limit_bytes=VMEM_LIMIT),
        name=name,
    )(o_f, fz, o_d, dz, ga, gb, x2d, w_bf, w_bd, w_o, g_final)


def _decode_kernel(pt_ref, fk_hbm, lf_hbm, dk_hbm, fv_hbm, dv_hbm,
                   fq_ref, dq_ref, fkn_ref, fvn_ref, dkn_ref, dvn_ref, lfn_ref, bias_ref, bias0_ref,
                   lq1_ref, lk1_ref, lq2_ref, lk2_ref, g_ref,
                   of_ref, od_ref,
                   buf_a, buf_b, buf_lf, sem, sf_sc, sd_sc, w4_sc, *, base, n_slots, n_pages, page, lam_init):
    b = pl.program_id(0)
    n_chunks = n_pages // n_slots
    width = fq_ref.shape[-1]
    n_diff = width // LANES
    hv = lax.broadcasted_iota(jnp.int32, (8, width), 0)
    lane = lax.broadcasted_iota(jnp.int32, (8, width), 1)
    own = (lane // HEAD_DIM) == hv
    a_f = jnp.where(own, fq_ref[0], 0.0).astype(BF16)
    a_d = jnp.where(own, dq_ref[0], 0.0).astype(BF16)

    n_buf = buf_a.shape[0]
    ahead = n_buf - 1
    n_seq = pl.num_programs(0)

    def chunk_copies(kinds, seq, chunk):
        slot = chunk % n_buf
        out = []
        for i in range(n_slots):
            pg = base + pt_ref[seq, chunk * n_slots + i]
            for src, dst, col in kinds:
                out.append(pltpu.make_async_copy(src.at[pg], dst.at[slot, i], sem.at[slot, col]))
        return out

    def start(kinds, seq, chunk):
        for cp in chunk_copies(kinds, seq, chunk):
            cp.start()

    k_kinds = ((fk_hbm, buf_a, 0), (dk_hbm, buf_b, 1), (lf_hbm, buf_lf, 2))
    v_kinds = ((fv_hbm, buf_a, 0), (dv_hbm, buf_b, 1))

    def pipelined(kinds, body, init, next_kinds, next_seq, next_ok):
        def step(chunk, carry):
            nxt = chunk + ahead

            @pl.when(nxt < n_chunks)
            def _():
                start(kinds, b, nxt)

            @pl.when(jnp.logical_and(nxt >= n_chunks, next_ok))
            def _():
                start(next_kinds, next_seq, nxt - n_chunks)

            for cp in chunk_copies(kinds, b, chunk):
                cp.wait()
            return body(chunk, chunk % n_buf, carry)

        return lax.fori_loop(0, n_chunks, step, init)

    @pl.when(b == 0)
    def _():
        for c in range(ahead):
            start(k_kinds, 0, c)

    u = lax.broadcasted_iota(jnp.int32, (page, page), 0)
    s = lax.broadcasted_iota(jnp.int32, (page, page), 1)
    tri = jnp.where(u <= s, 1.0, 0.0).astype(BF16)

    def key_chunk(chunk, slot, run):
        lf = buf_lf[slot].reshape(n_slots * 8, page)
        h1, h2, h3 = _split3(lf)
        local = (jnp.dot(h1, tri, preferred_element_type=F32) + jnp.dot(h2, tri, preferred_element_type=F32)
                 + jnp.dot(h3, tri, preferred_element_type=F32))
        for i in range(n_slots):
            pg = chunk * n_slots + i
            pre = local[8 * i:8 * i + 8] + run
            run = jnp.broadcast_to(pre[:, page - 1:page], run.shape)
            qk = jnp.dot(a_f, buf_a[slot, i].astype(BF16), preferred_element_type=F32)
            sf_sc[pg] = qk - pre
            sd_sc[pg] = jnp.dot(a_d, buf_b[slot, i].astype(BF16), preferred_element_type=F32) + bias_ref[pg]
        return run

    run = pipelined(k_kinds, key_chunk, jnp.zeros((8, page), F32), v_kinds, b, True)

    def softmax_rows(logits, self_logit):
        mx = jnp.max(jnp.max(logits, axis=0), axis=1, keepdims=True)
        mx = jnp.maximum(mx, self_logit)
        p = jnp.exp(logits - mx[None])
        p_self = jnp.exp(self_logit - mx)
        den = jnp.sum(jnp.sum(p, axis=0), axis=1, keepdims=True) + p_self
        return p / den[None], p_self / den

    def self_logit(a, kn_ref):
        kn = kn_ref[0].astype(BF16).astype(F32)
        return jnp.sum(a.astype(F32) * kn, axis=1, keepdims=True)

    c_new = run[:, 0:1] + lfn_ref[0]
    w, wf_self = softmax_rows(sf_sc[...] + c_new[None], self_logit(a_f, fkn_ref) + (c_new - c_new))
    sf_sc[...] = w
    w, wd_self = softmax_rows(sd_sc[...], self_logit(a_d, dkn_ref) + bias0_ref[:, 0:1])
    lam = _lam_value(lq1_ref[...], lk1_ref[...], lq2_ref[...], lk2_ref[...], lam_init)
    rows = n_pages * 8
    w2 = w.reshape(rows, page)
    row = lax.broadcasted_iota(jnp.int32, (rows, page), 0)
    wc = jnp.where(row % 2 == 0, w2 - lam * pltpu.roll(w2, rows - 1, 0), 0.0)
    wd8 = jnp.broadcast_to(wd_self, (8, page))
    wd_self = jnp.where(row[:8] % 2 == 0, wd8 - lam * pltpu.roll(wd8, 7, 0), 0.0)[:, 0:1]
    pos = lax.broadcasted_iota(jnp.int32, (page, page * n_diff), 0)
    col = lax.broadcasted_iota(jnp.int32, (page, page * n_diff), 1)
    rep = jnp.where(col // n_diff == pos, 1.0, 0.0).astype(BF16)
    spread = jnp.dot(wc.astype(BF16), rep, preferred_element_type=F32)
    rowx = lax.broadcasted_iota(jnp.int32, spread.shape, 0)
    colx = lax.broadcasted_iota(jnp.int32, spread.shape, 1)
    w4_sc[...] = jnp.where((colx % n_diff) * 2 == rowx % 8, spread, 0.0).reshape(w4_sc.shape)

    def value_chunk(chunk, slot, carry):
        acc_f, acc_d = carry
        for i in range(n_slots):
            pg = chunk * n_slots + i
            acc_f = acc_f + lax.dot_general(sf_sc[pg].astype(BF16), buf_a[slot, i].astype(BF16), _NT,
                                            preferred_element_type=F32)
            acc_d = acc_d + jnp.dot(w4_sc[pg].astype(BF16), buf_b[slot, i].astype(BF16),
                                    preferred_element_type=F32)
        return acc_f, acc_d

    acc_f, acc_d = pipelined(v_kinds, value_chunk, (jnp.zeros((8, width), F32), jnp.zeros((8, LANES), F32)),
                             k_kinds, b + 1, b + 1 < n_seq)

    acc = acc_f + wf_self * fvn_ref[0]
    of_ref[0] = jnp.sum(jnp.where(own, acc, 0.0), axis=0, keepdims=True)
    dvn = dvn_ref[0]
    zero = jnp.zeros((1, LANES), F32)
    vn8 = jnp.concatenate([piece for h in range(n_diff)
                           for piece in (dvn[:, h * LANES:(h + 1) * LANES], zero)], axis=0)
    od8 = acc_d + wd_self * vn8
    y8 = od8 * lax.rsqrt(jnp.mean(od8 * od8, axis=1, keepdims=True) + RMS_EPS)
    y8 = (y8 * g_ref[...]) * (1.0 - lam_init)
    od_ref[0] = jnp.concatenate([y8[2 * h:2 * h + 1] for h in range(n_diff)], axis=1)


def _decode_attention(page_table, base, fk, lft, dk, fv, dv, fq, dq, fkn, fvn, dkn, dvn, lfn, bias, bias0,
                      lq1, lk1, lq2, lk2, g, layer, *, n_slots, lam_init):
    bs, n_pages = page_table.shape
    width, page = fk.shape[1], fk.shape[2]
    n_buf = 4
    assert dv.shape[1:] == fk.shape[1:] and n_pages % (n_slots * n_buf) == 0

    row_spec = pl.BlockSpec((1, 1, width), lambda b, pt: (b, 0, 0))
    full = lambda a: pl.BlockSpec(a.shape, lambda b, pt: (0,) * a.ndim)
    in_specs = ([pl.BlockSpec(memory_space=pl.ANY)] * 5
                + [row_spec] * 6
                + [pl.BlockSpec((1, 8, 1), lambda b, pt: (b, 0, 0)), full(bias), full(bias0)]
                + [_layer_block(a, layer) for a in (lq1, lk1, lq2, lk2, g)])
    scratch = [pltpu.VMEM((n_buf, n_slots, width, page), F32), pltpu.VMEM((n_buf, n_slots, width, page), F32),
               pltpu.VMEM((n_buf, n_slots, 8, page), F32), pltpu.SemaphoreType.DMA((n_buf, 3)),
               pltpu.VMEM((n_pages, 8, page), F32), pltpu.VMEM((n_pages, 8, page), F32),
               pltpu.VMEM((n_pages, 8, dv.shape[1]), F32)]
    grid_spec = pltpu.PrefetchScalarGridSpec(
        num_scalar_prefetch=1, grid=(bs,), in_specs=in_specs,
        out_specs=[row_spec, row_spec], scratch_shapes=scratch)
    return pl.pallas_call(
        functools.partial(_decode_kernel, base=base, n_slots=n_slots, n_pages=n_pages, page=page,
                          lam_init=lam_init),
        grid_spec=grid_spec,
        out_shape=[jax.ShapeDtypeStruct((bs, 1, width), F32), jax.ShapeDtypeStruct((bs, 1, width), F32)],
        compiler_params=pltpu.CompilerParams(dimension_semantics=("arbitrary",), vmem_limit_bytes=VMEM_LIMIT),
        name="decode_attention",
    )(page_table, fk, lft, dk, fv, dv, fq, dq, fkn, fvn, dkn, dvn, lfn, bias, bias0, lq1, lk1, lq2, lk2, g)


def kernel(x_prompt, x_sample, cache_fox_k, cache_fox_v, cache_fox_logf, cache_diff_k, cache_diff_v, page_table,
           norm_g, w_in, b_forget, lambda_q1, lambda_k1, lambda_q2, lambda_k2, diff_subln_g, w_branch_fox,
           w_branch_diff, w_out, rel_bias, final_norm_g):
    depth = w_in.shape[0]
    b, s, d = x_prompt.shape
    bs = x_sample.shape[0]
    n_pool, page = cache_fox_k.shape[1], cache_fox_k.shape[2]
    n_pages = page_table.shape[1]
    past_len = n_pages * page
    n_fox = FOX_WIDTH // HEAD_DIM
    n_diff = DIFF_WIDTH // (2 * HEAD_DIM)
    tq = 256
    n_units = 4
    n_slots = 4

    w_t = jnp.swapaxes(w_in, 1, 2)
    w_main = jnp.concatenate([w_t[:, :3 * FOX_WIDTH], w_t[:, 3 * FOX_WIDTH + n_fox:]], axis=1).astype(BF16)
    w_f = jnp.pad(w_t[:, 3 * FOX_WIDTH:3 * FOX_WIDTH + n_fox], ((0, 0), (0, LANES - n_fox), (0, 0))).astype(BF16)
    b_f = jnp.pad(b_forget, ((0, 0), (0, LANES - n_fox)))
    w_bf = w_branch_fox.astype(BF16)
    w_bd = w_branch_diff.astype(BF16)
    w_o = w_out.astype(BF16)

    fk_c = cache_fox_k.transpose(0, 1, 3, 4, 2).reshape(depth * n_pool, FOX_WIDTH, page)
    fv_c = cache_fox_v.transpose(0, 1, 3, 4, 2).reshape(depth * n_pool, FOX_WIDTH, page)
    dk_c = cache_diff_k.transpose(0, 1, 3, 4, 5, 2).reshape(depth * n_pool, DIFF_WIDTH, page)
    dv_c = cache_diff_v.reshape(depth * n_pool, page * n_diff, 2 * HEAD_DIM)
    lf_c = cache_fox_logf.transpose(0, 1, 3, 2).reshape(depth * n_pool, n_fox, page)

    bias_p = _prompt_bias(rel_bias, tq, tq)
    bias_s, bias_s0 = _decode_bias(rel_bias, past_len, page)

    xp = x_prompt
    xs = x_sample.reshape(bs, d)
    stacked = (jnp.zeros((depth, b, FOX_WIDTH, s), F32), jnp.zeros((depth, b, FOX_WIDTH, s), F32),
               jnp.zeros((depth, b, n_fox, s), F32), jnp.zeros((depth, b, DIFF_WIDTH, s), F32),
               jnp.zeros((depth, b, s * n_diff, 2 * HEAD_DIM), F32))
    outs_s = [[] for _ in range(5)]
    g = norm_g[:, None, :]
    lvec = [a[:, None, :] for a in (lambda_q1, lambda_k1, lambda_q2, lambda_k2)]
    for l in range(depth):
        lam_init = _lambda_init(l)
        final = l == depth - 1

        (fqt, fkb, fvt, fz, dqt, dkb, dvt, dz, ga, gb, ct, c, *stacked) = _inproj_prompt(
            xp, g, w_main, w_f, b_f[:, :, None], tuple(stacked), l, depth, tm=tq)
        chunks = lambda a: a.reshape(b, s // tq, a.shape[1], tq)
        rows = lambda a: a.reshape(b, s, -1)
        n_streams = 2 * n_units
        cq = ct.reshape(b, n_fox // n_streams, n_streams, s)
        ck = c.reshape(b, s, n_fox // n_streams, n_streams).transpose(0, 2, 1, 3)
        o_f = _fox_attention(chunks(fqt), rows(fkb), chunks(fvt), cq, ck, tq=tq, n_units=n_units)
        o_d = _diff_attention(chunks(dqt), rows(dkb), chunks(dvt), bias_p, *lvec, diff_subln_g[:, :, None], l,
                              tq=tq, n_units=n_units, lam_init=lam_init)
        xp = _merge(o_f.reshape(b * s, -1), fz, o_d.reshape(b * s, -1), dz, ga, gb, xp.reshape(b * s, d),
                    w_bf, w_bd, w_o, final_norm_g[None], l, tm=2 * tq, final=final,
                    name="merge_prompt").reshape(b, s, d)

        sq, sk, sv, sz, sdq, sdk, sdv, sdz, sga, sgb, slf = _inproj_sample(xs, g, w_main, w_f, b_f[:, None, :], l)
        row3 = lambda a: a.reshape(bs, 1, -1)
        o_fs, o_ds = _decode_attention(
            page_table, l * n_pool, fk_c, lf_c, dk_c, fv_c, dv_c,
            row3(sq), row3(sdq), row3(sk), row3(sv), row3(sdk), row3(sdv), slf.reshape(bs, n_fox, 1),
            bias_s, bias_s0, *lvec, diff_subln_g[:, None, :], l, n_slots=n_slots, lam_init=lam_init)
        xs = _merge(o_fs.reshape(bs, -1), sz, o_ds.reshape(bs, -1), sdz, sga, sgb, xs, w_bf, w_bd, w_o,
                    final_norm_g[None], l, tm=bs, final=final, name="merge_sample")
        for acc, a in zip(outs_s, (sk, sv, slf, sdk, sdv)):
            acc.append(a)

    pk, pv, plf, pdk, pdv = stacked
    sk_l, sv_l, slf_l, sdk_l, sdv_l = [jnp.stack(a) for a in outs_s]
    return (xp, xs.reshape(bs, 1, d),
            pk.reshape(depth, b, n_fox, HEAD_DIM, s).transpose(0, 1, 4, 2, 3),
            pv.reshape(depth, b, n_fox, HEAD_DIM, s).transpose(0, 1, 4, 2, 3),
            plf.transpose(0, 1, 3, 2),
            pdk.reshape(depth, b, n_diff, 2, HEAD_DIM, s).transpose(0, 1, 5, 2, 3, 4),
            pdv.reshape(depth, b, s, n_diff, 2 * HEAD_DIM),
            sk_l.reshape(depth, bs, 1, n_fox, HEAD_DIM), sv_l.reshape(depth, bs, 1, n_fox, HEAD_DIM),
            slf_l.reshape(depth, bs, 1, n_fox),
            sdk_l.reshape(depth, bs, 1, n_diff, 2, HEAD_DIM), sdv_l.reshape(depth, bs, 1, n_diff, 2 * HEAD_DIM))
```

```python
import functools
import math

import jax
import jax.numpy as jnp
from jax import lax
from jax.experimental import pallas as pl
from jax.experimental.pallas import tpu as pltpu

HEAD_DIM = 64
LANES = 128
FOX_WIDTH = 512
DIFF_WIDTH = 512
NUM_BUCKETS = 32
MAX_DISTANCE = 128
RMS_EPS = 1e-6
NEG_INF = -1e30
SCALE = HEAD_DIM ** -0.5
LOG2E = math.log2(math.e)
SUM_ROWS = 16
VMEM_LIMIT = 56 * 1024 * 1024

F32 = jnp.float32
BF16 = jnp.bfloat16

_NT = (((1,), (1,)), ((), ()))


def _lambda_init(layer):
    return 0.8 - 0.6 * math.exp(-0.3 * layer)


def _sigmoid(x):
    return 1.0 / (1.0 + jnp.exp(-x))


def _log_sigmoid(x):
    return jnp.minimum(x, 0.0) - jnp.log1p(jnp.exp(-jnp.abs(x)))


def _split3(x):
    h1 = x.astype(BF16)
    r1 = x - h1.astype(F32)
    h2 = r1.astype(BF16)
    r2 = r1 - h2.astype(F32)
    return h1, h2, r2.astype(BF16)


def _prefix_sums(x, n):
    u = lax.broadcasted_iota(jnp.int32, (n, n), 0)
    s = lax.broadcasted_iota(jnp.int32, (n, n), 1)
    tri = jnp.where(u <= s, 1.0, 0.0).astype(BF16)
    h1, h2, h3 = _split3(x)
    return (jnp.dot(h1, tri, preferred_element_type=F32) + jnp.dot(h2, tri, preferred_element_type=F32)
            + jnp.dot(h3, tri, preferred_element_type=F32))


def _t5_bucket(rel):
    n = jnp.maximum(rel, 0)
    max_exact = NUM_BUCKETS // 2
    nf = jnp.maximum(n, 1).astype(F32)
    large = max_exact + (jnp.log(nf / max_exact) / math.log(MAX_DISTANCE / max_exact)
                         * (NUM_BUCKETS - max_exact)).astype(jnp.int32)
    large = jnp.minimum(large, NUM_BUCKETS - 1)
    return jnp.where(n < max_exact, n, large)


def _lam_value(lq1, lk1, lq2, lk2, lam_init):
    a = jnp.sum(lq1 * lk1, axis=1, keepdims=True)
    b = jnp.sum(lq2 * lk2, axis=1, keepdims=True)
    return jnp.exp(a) - jnp.exp(b) + lam_init


def _layer_block(a, layer, single_buffer=False):
    idx = (layer,) + (0,) * (a.ndim - 1)
    kwargs = dict(pipeline_mode=pl.Buffered(1)) if single_buffer else {}
    return pl.BlockSpec((None,) + a.shape[1:], lambda *_: idx, **kwargs)


def _rms_rows(x, g_ref):
    y = x * lax.rsqrt(jnp.mean(x * x, axis=-1, keepdims=True) + RMS_EPS)
    return y * g_ref[...]


_FQ, _FK, _FV, _FZ, _DQ, _DK, _DV, _DZ, _GA, _GB = (0, 512, 1024, 1536, 2048, 2560, 3072, 3584, 4096, 5120)


def _inproj_prompt_kernel(*refs, n_alias):
    x_ref, g_ref, w_ref, wf_ref, bf_ref = refs[:5]
    (fqt_ref, fk_ref, fvt_ref, fz_ref, dqt_ref, dk_ref, dvt_ref, dz_ref, ga_ref, gb_ref, ct_ref, c_ref,
     pk_ref, pv_ref, plf_ref, pdk_ref, pdv_ref, carry_ref) = refs[5 + n_alias:]
    hb = _rms_rows(x_ref[...], g_ref).astype(BF16)
    tm = hb.shape[0]

    def put(ref, val):
        if n_alias:
            ref[...] = val
        else:
            ref[0] = val
            ref[1:] = jnp.zeros((ref.shape[0] - 1,) + val.shape, val.dtype)

    def tr(r0, n):
        return lax.dot_general(w_ref[r0:r0 + n, :], hb, _NT, preferred_element_type=F32)

    def nat(r0, n):
        return lax.dot_general(hb, w_ref[r0:r0 + n, :], _NT, preferred_element_type=F32)

    fqt_ref[0] = (tr(_FQ, 512) * (SCALE * LOG2E)).astype(BF16)
    kt = tr(_FK, 512)
    put(pk_ref, kt)
    fk_ref[...] = kt.T.astype(BF16)
    vt = tr(_FV, 512)
    put(pv_ref, vt)
    fvt_ref[0] = vt.astype(BF16)
    fz_ref[...] = nat(_FZ, 512).astype(BF16)
    dqt_ref[0] = (tr(_DQ, 512) * (SCALE * LOG2E)).astype(BF16)
    kt = tr(_DK, 512)
    put(pdk_ref, kt)
    dk_ref[...] = kt.T.astype(BF16)
    v = nat(_DV, 512)
    n_dv = v.shape[1] // LANES
    for h in range(n_dv):
        if n_alias:
            pdv_ref[pl.ds(h, tm, stride=n_dv), :] = v[:, h * LANES:(h + 1) * LANES]
        else:
            pdv_ref[0, pl.ds(h, tm, stride=n_dv), :] = v[:, h * LANES:(h + 1) * LANES]
    if not n_alias:
        pdv_ref[1:] = jnp.zeros((pdv_ref.shape[0] - 1,) + pdv_ref.shape[1:], F32)
    dvt_ref[0] = v.T.astype(BF16)
    dz_ref[...] = nat(_DZ, 512).astype(BF16)
    ga_ref[...] = nat(_GA, 1024).astype(BF16)
    gb_ref[...] = nat(_GB, 1024).astype(BF16)

    fft = lax.dot_general(wf_ref[...], hb, _NT, preferred_element_type=F32) + bf_ref[...]
    lft = _log_sigmoid(fft)
    put(plf_ref, lft[:8])

    @pl.when(pl.program_id(1) == 0)
    def _():
        carry_ref[...] = jnp.zeros_like(carry_ref)

    ct = _prefix_sums(lft, tm) + carry_ref[...]
    carry_ref[...] = ct[:, tm - 1:tm]
    ct_ref[...] = ct[:8]
    c_ref[...] = ct.T[:, :8]


def _inproj_prompt(x3d, g, w_t, wf_t, bf_col, stacked, layer, depth, *, tm):
    b, s, d = x3d.shape
    tps = s // tm
    n_alias = len(stacked)
    im2 = lambda bi, i: (bi * tps + i, 0)
    im3 = lambda bi, i: (bi * tps + i, 0, 0)
    m = b * s
    out_shape = [
        jax.ShapeDtypeStruct((m // tm, 512, tm), BF16),
        jax.ShapeDtypeStruct((m, 512), BF16),
        jax.ShapeDtypeStruct((m // tm, 512, tm), BF16),
        jax.ShapeDtypeStruct((m, 512), BF16),
        jax.ShapeDtypeStruct((m // tm, 512, tm), BF16),
        jax.ShapeDtypeStruct((m, 512), BF16),
        jax.ShapeDtypeStruct((m // tm, 512, tm), BF16),
        jax.ShapeDtypeStruct((m, 512), BF16),
        jax.ShapeDtypeStruct((m, 1024), BF16),
        jax.ShapeDtypeStruct((m, 1024), BF16),
        jax.ShapeDtypeStruct((b, 8, s), F32),
        jax.ShapeDtypeStruct((m, 8), F32),
        jax.ShapeDtypeStruct((depth, b, 512, s), F32),
        jax.ShapeDtypeStruct((depth, b, 512, s), F32),
        jax.ShapeDtypeStruct((depth, b, 8, s), F32),
        jax.ShapeDtypeStruct((depth, b, 512, s), F32),
        jax.ShapeDtypeStruct((depth, b, s * 4, LANES), F32),
    ]
    t_spec = pl.BlockSpec((1, 512, tm), im3)
    r_spec = lambda width: pl.BlockSpec((tm, width), im2)
    lead = None if n_alias else depth
    st_spec = lambda rows: pl.BlockSpec((lead, None, rows, tm), lambda bi, i: (layer, bi, 0, i))
    out_specs = [t_spec, r_spec(512), t_spec, r_spec(512), t_spec, r_spec(512), t_spec, r_spec(512),
                 r_spec(1024), r_spec(1024),
                 pl.BlockSpec((None, 8, tm), lambda bi, i: (bi, 0, i)), r_spec(8),
                 st_spec(512), st_spec(512), st_spec(8), st_spec(512),
                 pl.BlockSpec((lead, None, tm * 4, LANES), lambda bi, i: (layer, bi, i, 0))]
    in_specs = ([pl.BlockSpec((None, tm, d), lambda bi, i: (bi, i, 0))]
                + [_layer_block(a, layer, single_buffer=True) for a in (g, w_t, wf_t, bf_col)]
                + [pl.BlockSpec(memory_space=pl.ANY)] * n_alias)
    return pl.pallas_call(
        functools.partial(_inproj_prompt_kernel, n_alias=n_alias),
        grid=(b, tps),
        in_specs=in_specs,
        out_specs=out_specs,
        out_shape=out_shape,
        scratch_shapes=[pltpu.VMEM((LANES, 1), F32)],
        input_output_aliases={5 + k: 12 + k for k in range(n_alias)},
        compiler_params=pltpu.CompilerParams(dimension_semantics=("arbitrary", "arbitrary"),
                                             vmem_limit_bytes=VMEM_LIMIT),
        name="inproj_prompt",
    )(x3d, g, w_t, wf_t, bf_col, *stacked)


def _inproj_sample_kernel(x_ref, g_ref, w_ref, wf_ref, bf_ref,
                          fq_ref, fk_ref, fv_ref, fz_ref, dq_ref, dk_ref, dv_ref, dz_ref, ga_ref, gb_ref, lf_ref):
    hb = _rms_rows(x_ref[...], g_ref).astype(BF16)

    def nat(r0, n):
        return lax.dot_general(hb, w_ref[r0:r0 + n, :], _NT, preferred_element_type=F32)

    fq_ref[...] = nat(_FQ, 512) * SCALE
    fk_ref[...] = nat(_FK, 512)
    fv_ref[...] = nat(_FV, 512)
    fz_ref[...] = nat(_FZ, 512)
    dq_ref[...] = nat(_DQ, 512) * SCALE
    dk_ref[...] = nat(_DK, 512)
    dv_ref[...] = nat(_DV, 512)
    dz_ref[...] = nat(_DZ, 512)
    ga_ref[...] = nat(_GA, 1024)
    gb_ref[...] = nat(_GB, 1024)
    ff = lax.dot_general(hb, wf_ref[...], _NT, preferred_element_type=F32) + bf_ref[...]
    lf_ref[...] = _log_sigmoid(ff)[:, :8]


def _inproj_sample(x2d, g, w_t, wf_t, bf_row, layer):
    m, d = x2d.shape
    full = lambda shape: pl.BlockSpec(shape, lambda i: (0,) * len(shape))
    widths = [512] * 8 + [1024, 1024, 8]
    return pl.pallas_call(
        _inproj_sample_kernel,
        grid=(1,),
        in_specs=[full((m, d))] + [_layer_block(a, layer) for a in (g, w_t, wf_t, bf_row)],
        out_specs=[full((m, w)) for w in widths],
        out_shape=[jax.ShapeDtypeStruct((m, w), F32) for w in widths],
        compiler_params=pltpu.CompilerParams(dimension_semantics=("arbitrary",), vmem_limit_bytes=VMEM_LIMIT),
        name="inproj_sample",
    )(x2d, g, w_t, wf_t, bf_row)


def _bias_lookup(rb_ref, bucket, head):
    out = jnp.zeros(bucket.shape, F32)
    for b in range(NUM_BUCKETS):
        out = jnp.where(bucket == b, rb_ref[b, head], out)
    return out


def _prompt_bias_kernel(rb_ref, out_ref, *, tq, tk):
    h = pl.program_id(0)
    key = lax.broadcasted_iota(jnp.int32, (tk, tq), 0)
    qry = lax.broadcasted_iota(jnp.int32, (tk, tq), 1)
    for t in range(3):
        bucket = _t5_bucket(qry - key + t * tk)
        out_ref[0, t] = _bias_lookup(rb_ref, bucket, h) * LOG2E


def _prompt_bias(rel_bias, tq, tk):
    return pl.pallas_call(
        functools.partial(_prompt_bias_kernel, tq=tq, tk=tk),
        grid=(4,),
        in_specs=[pl.BlockSpec(memory_space=pltpu.SMEM)],
        out_specs=pl.BlockSpec((1, 3, tk, tq), lambda h: (h, 0, 0, 0)),
        out_shape=jax.ShapeDtypeStruct((4, 3, tk, tq), F32),
        name="prompt_bias",
    )(rel_bias)


def _decode_bias_kernel(rb_ref, out_ref, self_ref, *, past_len, page):
    p = pl.program_id(0)
    pos = p * page + lax.broadcasted_iota(jnp.int32, (8, page), 1)
    row = lax.broadcasted_iota(jnp.int32, (8, page), 0)
    bucket = _t5_bucket(past_len - pos)
    bucket0 = _t5_bucket(jnp.zeros((8, page), jnp.int32))
    acc = jnp.zeros((8, page), F32)
    acc0 = jnp.zeros((8, page), F32)
    for hh in range(4):
        sel = (row // 2) == hh
        acc = jnp.where(sel, _bias_lookup(rb_ref, bucket, hh), acc)
        acc0 = jnp.where(sel, _bias_lookup(rb_ref, bucket0, hh), acc0)
    out_ref[0] = acc
    self_ref[...] = acc0


def _decode_bias(rel_bias, past_len, page):
    n_pages = past_len // page
    return pl.pallas_call(
        functools.partial(_decode_bias_kernel, past_len=past_len, page=page),
        grid=(n_pages,),
        in_specs=[pl.BlockSpec(memory_space=pltpu.SMEM)],
        out_specs=[pl.BlockSpec((1, 8, page), lambda p: (p, 0, 0)), pl.BlockSpec((8, page), lambda p: (0, 0))],
        out_shape=[jax.ShapeDtypeStruct((n_pages, 8, page), F32), jax.ShapeDtypeStruct((8, page), F32)],
        name="decode_bias",
    )(rel_bias)


def _attn_chunk(j, n_sub, masked, qs, k_ref, vt_ref, sub_term, add_term, col_terms, m_sc, acc_sc, tk):
    start = pl.multiple_of(j * tk, tk)
    size = n_sub * tk
    tq = qs[0].shape[1]
    if masked:
        key = lax.broadcasted_iota(jnp.int32, (size, tq), 0)
        qry = lax.broadcasted_iota(jnp.int32, (size, tq), 1)
        keep = key <= qry
    ts = []
    for n, qn in enumerate(qs):
        u = n // 2
        ks = k_ref[0, pl.ds(start, size), u * LANES:(u + 1) * LANES]
        t = jnp.dot(ks, qn, preferred_element_type=F32)
        if sub_term is not None:
            t = t - sub_term(n, start, size)
        if add_term is not None:
            t = t + jnp.concatenate([add_term(n)] * n_sub, axis=0)
        if masked:
            t = jnp.where(keep, t, NEG_INF)
        ts.append(t)
    ones = jnp.ones((SUM_ROWS, size), BF16)
    for n, t in enumerate(ts):
        u = n // 2
        ct = col_terms[n]
        mcol = jnp.max(t, axis=0, keepdims=True)
        if ct is not None:
            mcol = mcol + ct
        m_old = m_sc[n]
        m_new = jnp.maximum(m_old, mcol)
        shift = m_new if ct is None else m_new - ct
        p = jnp.exp2(t - shift).astype(BF16)
        alpha = jnp.exp2(m_old - m_new)
        m_sc[n] = m_new
        vts = jnp.concatenate([vt_ref[0, j + i, u * LANES:(u + 1) * LANES, :] for i in range(n_sub)], axis=1)
        vts = jnp.concatenate([vts, ones], axis=0)
        acc_sc[n] = alpha * acc_sc[n] + jnp.dot(vts, p, preferred_element_type=F32)


def _for_key_chunks(n_chunks, update):
    def pair(jp, carry):
        update(2 * jp, 2)
        return carry

    lax.fori_loop(0, n_chunks // 2, pair, 0)

    @pl.when(n_chunks % 2 == 1)
    def _():
        update(n_chunks - 1, 1)


def _attn_init(m_sc, acc_sc):
    m_sc[...] = jnp.full(m_sc.shape, -jnp.inf, F32)
    acc_sc[...] = jnp.zeros_like(acc_sc)


def _attn_result(acc_sc, n):
    return acc_sc[n, :LANES, :] / acc_sc[n, LANES:LANES + 1, :]


def _split_q(qt_ref, n_units):
    qs = []
    for u in range(n_units):
        qt = qt_ref[0, 0, u * LANES:(u + 1) * LANES, :]
        row = lax.broadcasted_iota(jnp.int32, qt.shape, 0)
        zero = jnp.zeros_like(qt)
        qs += [jnp.where(row < HEAD_DIM, qt, zero), jnp.where(row >= HEAD_DIM, qt, zero)]
    return qs


def _fox_attn_kernel(qt_ref, k_ref, vt_ref, cq_ref, ck_ref, o_ref, ckb_sc, m_sc, acc_sc, *, tk, n_units):
    qi = pl.program_id(2)
    n_streams = 2 * n_units

    @pl.when(qi == 0)
    def _():
        ck = ck_ref[0, 0] * LOG2E
        for n in range(n_streams):
            ckb_sc[n] = jnp.broadcast_to(ck[:, n:n + 1], ckb_sc.shape[1:])

    qs = _split_q(qt_ref, n_units)
    cq = cq_ref[0, 0] * LOG2E
    col_terms = [cq[n:n + 1, :] for n in range(n_streams)]
    _attn_init(m_sc, acc_sc)

    def key_decay(n, start, size):
        ckn = ckb_sc[n, pl.ds(start, size), :]
        return jnp.concatenate([ckn] * (qs[0].shape[1] // LANES), axis=1)

    def step(j, n_sub, masked):
        _attn_chunk(j, n_sub, masked, qs, k_ref, vt_ref, key_decay, None, col_terms, m_sc, acc_sc, tk)

    _for_key_chunks(qi, lambda j, n_sub: step(j, n_sub, False))
    step(qi, 1, True)
    row = lax.broadcasted_iota(jnp.int32, (LANES, qs[0].shape[1]), 0)
    for u in range(n_units):
        ot = jnp.where(row < HEAD_DIM, _attn_result(acc_sc, 2 * u), _attn_result(acc_sc, 2 * u + 1))
        o_ref[0, :, u * LANES:(u + 1) * LANES] = ot.T.astype(o_ref.dtype)


def _diff_attn_kernel(qt_ref, k_ref, vt_ref, bias_ref, lq1_ref, lk1_ref, lq2_ref, lk2_ref, g_ref,
                      o_ref, m_sc, acc_sc, *, tk, n_units, lam_init):
    qi = pl.program_id(2)
    n_streams = 2 * n_units
    qs = _split_q(qt_ref, n_units)
    _attn_init(m_sc, acc_sc)

    def step(j, n_sub, tile, masked):
        _attn_chunk(j, n_sub, masked, qs, k_ref, vt_ref, None, lambda n: bias_ref[n // 2, tile],
                    [None] * n_streams, m_sc, acc_sc, tk)

    _for_key_chunks(jnp.maximum(qi - 1, 0), lambda j, n_sub: step(j, n_sub, 2, False))

    @pl.when(qi >= 1)
    def _():
        step(qi - 1, 1, 1, False)

    step(qi, 1, 0, True)
    lam = _lam_value(lq1_ref[...], lk1_ref[...], lq2_ref[...], lk2_ref[...], lam_init)
    for u in range(n_units):
        ot = _attn_result(acc_sc, 2 * u) - lam * _attn_result(acc_sc, 2 * u + 1)
        y = ot * lax.rsqrt(jnp.mean(ot * ot, axis=0, keepdims=True) + RMS_EPS)
        o_ref[0, :, u * LANES:(u + 1) * LANES] = ((y * g_ref[...]) * (1.0 - lam_init)).T.astype(o_ref.dtype)


def _attn_common(b, s, tq, n_units):
    width = n_units * LANES
    nq = s // tq
    grid = (b, FOX_WIDTH // width, nq)
    qt_spec = pl.BlockSpec((1, 1, width, tq), lambda bi, g, qi: (bi, qi, g, 0))
    k_spec = pl.BlockSpec((1, s, width), lambda bi, g, qi: (bi, 0, g))
    vt_spec = pl.BlockSpec((1, nq, width, tq), lambda bi, g, qi: (bi, 0, g, 0))
    o_spec = pl.BlockSpec((1, tq, width), lambda bi, g, qi: (bi, qi, g))
    n_streams = 2 * n_units
    scratch = [pltpu.VMEM((n_streams, 1, tq), F32), pltpu.VMEM((n_streams, LANES + SUM_ROWS, tq), F32)]
    params = pltpu.CompilerParams(dimension_semantics=("arbitrary", "arbitrary", "arbitrary"),
                                  vmem_limit_bytes=VMEM_LIMIT)
    return grid, qt_spec, k_spec, vt_spec, o_spec, scratch, params


def _fox_attention(qt, kb, vt, cq, ck, *, tq, n_units):
    b, s, _ = kb.shape
    grid, qt_spec, k_spec, vt_spec, o_spec, scratch, params = _attn_common(b, s, tq, n_units)
    n_streams = 2 * n_units
    return pl.pallas_call(
        functools.partial(_fox_attn_kernel, tk=tq, n_units=n_units),
        grid=grid,
        in_specs=[qt_spec, k_spec, vt_spec,
                  pl.BlockSpec((1, 1, n_streams, tq), lambda bi, g, qi: (bi, g, 0, qi)),
                  pl.BlockSpec((1, 1, s, n_streams), lambda bi, g, qi: (bi, g, 0, 0))],
        out_specs=o_spec,
        out_shape=jax.ShapeDtypeStruct((b, s, FOX_WIDTH), BF16),
        scratch_shapes=[pltpu.VMEM((n_streams, s, LANES), F32)] + scratch,
        compiler_params=params,
        name="fox_attention",
    )(qt, kb, vt, cq, ck)


def _diff_attention(qt, kb, vt, bias, lq1, lk1, lq2, lk2, g_col, layer, *, tq, n_units, lam_init):
    b, s, _ = kb.shape
    grid, qt_spec, k_spec, vt_spec, o_spec, scratch, params = _attn_common(b, s, tq, n_units)
    return pl.pallas_call(
        functools.partial(_diff_attn_kernel, tk=tq, n_units=n_units, lam_init=lam_init),
        grid=grid,
        in_specs=[qt_spec, k_spec, vt_spec,
                  pl.BlockSpec((n_units, 3, tq, tq), lambda bi, g, qi: (g, 0, 0, 0))]
                 + [_layer_block(a, layer) for a in (lq1, lk1, lq2, lk2, g_col)],
        out_specs=o_spec,
        out_shape=jax.ShapeDtypeStruct((b, s, DIFF_WIDTH), BF16),
        scratch_shapes=scratch,
        compiler_params=params,
        name="diff_attention",
    )(qt, kb, vt, bias, lq1, lk1, lq2, lk2, g_col)


def _merge_kernel(of_ref, fz_ref, od_ref, dz_ref, ga_ref, gb_ref, x_ref, wbf_ref, wbd_ref, wo_ref, g_ref,
                  out_ref, *, final):
    fz = fz_ref[...].astype(F32)
    dz = dz_ref[...].astype(F32)
    a = (of_ref[...].astype(F32) * (fz * _sigmoid(fz))).astype(BF16)
    d = (od_ref[...].astype(F32) * (dz * _sigmoid(dz))).astype(BF16)
    y_fox = jnp.dot(a, wbf_ref[...], preferred_element_type=F32)
    y_diff = jnp.dot(d, wbd_ref[...], preferred_element_type=F32)
    m = _sigmoid(ga_ref[...].astype(F32)) * y_fox + _sigmoid(gb_ref[...].astype(F32)) * y_diff
    x = x_ref[...] + jnp.dot(m.astype(BF16), wo_ref[...], preferred_element_type=F32)
    if final:
        x = _rms_rows(x, g_ref)
    out_ref[...] = x


def _merge(o_f, fz, o_d, dz, ga, gb, x2d, w_bf, w_bd, w_o, g_final, layer, *, tm, final, name):
    m, d = x2d.shape
    row_spec = lambda width: pl.BlockSpec((tm, width), lambda i: (i, 0))
    return pl.pallas_call(
        functools.partial(_merge_kernel, final=final),
        grid=(m // tm,),
        in_specs=[row_spec(512), row_spec(512), row_spec(512), row_spec(512), row_spec(d), row_spec(d),
                  row_spec(d)] + [_layer_block(a, layer, single_buffer=True) for a in (w_bf, w_bd, w_o)]
                 + [pl.BlockSpec((1, d), lambda i: (0, 0), pipeline_mode=pl.Buffered(1))],
        out_specs=row_spec(d),
        out_shape=jax.ShapeDtypeStruct((m, d), F32),
        compiler_params=pltpu.CompilerParams(dimension_semantics=("arbitrary",), vmem_limit_bytes=VMEM_LIMIT),
        name=name,
    )(o_f, fz, o_d, dz, ga, gb, x2d, w_bf, w_bd, w_o, g_final)


def _decode_kernel(pt_ref, fk_hbm, lf_hbm, dk_hbm, fv_hbm, dv_hbm,
                   fq_ref, dq_ref, fkn_ref, fvn_ref, dkn_ref, dvn_ref, lfn_ref, bias_ref, bias0_ref,
                   lq1_ref, lk1_ref, lq2_ref, lk2_ref, g_ref,
                   of_ref, od_ref,
                   buf_a, buf_b, buf_lf, sem, sf_sc, sd_sc, w4_sc, *, base, n_slots, n_pages, page, lam_init):
    b = pl.program_id(0)
    n_chunks = n_pages // n_slots
    width = fq_ref.shape[-1]
    n_diff = width // LANES
    hv = lax.broadcasted_iota(jnp.int32, (8, width), 0)
    lane = lax.broadcasted_iota(jnp.int32, (8, width), 1)
    own = (lane // HEAD_DIM) == hv
    a_f = jnp.where(own, fq_ref[0], 0.0).astype(BF16)
    a_d = jnp.where(own, dq_ref[0], 0.0).astype(BF16)

    n_buf = buf_a.shape[0]
    ahead = n_buf - 1
    n_seq = pl.num_programs(0)

    def chunk_copies(kinds, seq, chunk):
        slot = chunk % n_buf
        out = []
        for i in range(n_slots):
            pg = base + pt_ref[seq, chunk * n_slots + i]
            for src, dst, col in kinds:
                out.append(pltpu.make_async_copy(src.at[pg], dst.at[slot, i], sem.at[slot, col]))
        return out

    def start(kinds, seq, chunk):
        for cp in chunk_copies(kinds, seq, chunk):
            cp.start()

    k_kinds = ((fk_hbm, buf_a, 0), (dk_hbm, buf_b, 1), (lf_hbm, buf_lf, 2))
    v_kinds = ((fv_hbm, buf_a, 0), (dv_hbm, buf_b, 1))

    def pipelined(kinds, body, init, next_kinds, next_seq, next_ok):
        def step(chunk, carry):
            nxt = chunk + ahead

            @pl.when(nxt < n_chunks)
            def _():
                start(kinds, b, nxt)

            @pl.when(jnp.logical_and(nxt >= n_chunks, next_ok))
            def _():
                start(next_kinds, next_seq, nxt - n_chunks)

            for cp in chunk_copies(kinds, b, chunk):
                cp.wait()
            return body(chunk, chunk % n_buf, carry)

        return lax.fori_loop(0, n_chunks, step, init)

    @pl.when(b == 0)
    def _():
        for c in range(ahead):
            start(k_kinds, 0, c)

    u = lax.broadcasted_iota(jnp.int32, (page, page), 0)
    s = lax.broadcasted_iota(jnp.int32, (page, page), 1)
    tri = jnp.where(u <= s, 1.0, 0.0).astype(BF16)

    def key_chunk(chunk, slot, run):
        lf = buf_lf[slot].reshape(n_slots * 8, page)
        h1, h2, h3 = _split3(lf)
        local = (jnp.dot(h1, tri, preferred_element_type=F32) + jnp.dot(h2, tri, preferred_element_type=F32)
                 + jnp.dot(h3, tri, preferred_element_type=F32))
        for i in range(n_slots):
            pg = chunk * n_slots + i
            pre = local[8 * i:8 * i + 8] + run
            run = jnp.broadcast_to(pre[:, page - 1:page], run.shape)
            qk = jnp.dot(a_f, buf_a[slot, i].astype(BF16), preferred_element_type=F32)
            sf_sc[pg] = qk - pre
            sd_sc[pg] = jnp.dot(a_d, buf_b[slot, i].astype(BF16), preferred_element_type=F32) + bias_ref[pg]
        return run

    run = pipelined(k_kinds, key_chunk, jnp.zeros((8, page), F32), v_kinds, b, True)

    def softmax_rows(logits, self_logit):
        mx = jnp.max(jnp.max(logits, axis=0), axis=1, keepdims=True)
        mx = jnp.maximum(mx, self_logit)
        p = jnp.exp(logits - mx[None])
        p_self = jnp.exp(self_logit - mx)
        den = jnp.sum(jnp.sum(p, axis=0), axis=1, keepdims=True) + p_self
        return p / den[None], p_self / den

    def self_logit(a, kn_ref):
        kn = kn_ref[0].astype(BF16).astype(F32)
        return jnp.sum(a.astype(F32) * kn, axis=1, keepdims=True)

    c_new = run[:, 0:1] + lfn_ref[0]
    w, wf_self = softmax_rows(sf_sc[...] + c_new[None], self_logit(a_f, fkn_ref) + (c_new - c_new))
    sf_sc[...] = w
    w, wd_self = softmax_rows(sd_sc[...], self_logit(a_d, dkn_ref) + bias0_ref[:, 0:1])
    lam = _lam_value(lq1_ref[...], lk1_ref[...], lq2_ref[...], lk2_ref[...], lam_init)
    rows = n_pages * 8
    w2 = w.reshape(rows, page)
    row = lax.broadcasted_iota(jnp.int32, (rows, page), 0)
    wc = jnp.where(row % 2 == 0, w2 - lam * pltpu.roll(w2, rows - 1, 0), 0.0)
    wd8 = jnp.broadcast_to(wd_self, (8, page))
    wd_self = jnp.where(row[:8] % 2 == 0, wd8 - lam * pltpu.roll(wd8, 7, 0), 0.0)[:, 0:1]
    pos = lax.broadcasted_iota(jnp.int32, (page, page * n_diff), 0)
    col = lax.broadcasted_iota(jnp.int32, (page, page * n_diff), 1)
    rep = jnp.where(col // n_diff == pos, 1.0, 0.0).astype(BF16)
    spread = jnp.dot(wc.astype(BF16), rep, preferred_element_type=F32)
    rowx = lax.broadcasted_iota(jnp.int32, spread.shape, 0)
    colx = lax.broadcasted_iota(jnp.int32, spread.shape, 1)
    w4_sc[...] = jnp.where((colx % n_diff) * 2 == rowx % 8, spread, 0.0).reshape(w4_sc.shape)

    def value_chunk(chunk, slot, carry):
        acc_f, acc_d = carry
        for i in range(n_slots):
            pg = chunk * n_slots + i
            acc_f = acc_f + lax.dot_general(sf_sc[pg].astype(BF16), buf_a[slot, i].astype(BF16), _NT,
                                            preferred_element_type=F32)
            acc_d = acc_d + jnp.dot(w4_sc[pg].astype(BF16), buf_b[slot, i].astype(BF16),
                                    preferred_element_type=F32)
        return acc_f, acc_d

    acc_f, acc_d = pipelined(v_kinds, value_chunk, (jnp.zeros((8, width), F32), jnp.zeros((8, LANES), F32)),
                             k_kinds, b + 1, b + 1 < n_seq)

    acc = acc_f + wf_self * fvn_ref[0]
    of_ref[0] = jnp.sum(jnp.where(own, acc, 0.0), axis=0, keepdims=True)
    dvn = dvn_ref[0]
    zero = jnp.zeros((1, LANES), F32)
    vn8 = jnp.concatenate([piece for h in range(n_diff)
                           for piece in (dvn[:, h * LANES:(h + 1) * LANES], zero)], axis=0)
    od8 = acc_d + wd_self * vn8
    y8 = od8 * lax.rsqrt(jnp.mean(od8 * od8, axis=1, keepdims=True) + RMS_EPS)
    y8 = (y8 * g_ref[...]) * (1.0 - lam_init)
    od_ref[0] = jnp.concatenate([y8[2 * h:2 * h + 1] for h in range(n_diff)], axis=1)


def _decode_attention(page_table, base, fk, lft, dk, fv, dv, fq, dq, fkn, fvn, dkn, dvn, lfn, bias, bias0,
                      lq1, lk1, lq2, lk2, g, layer, *, n_slots, lam_init):
    bs, n_pages = page_table.shape
    width, page = fk.shape[1], fk.shape[2]
    n_buf = 4
    assert dv.shape[1:] == fk.shape[1:] and n_pages % (n_slots * n_buf) == 0

    row_spec = pl.BlockSpec((1, 1, width), lambda b, pt: (b, 0, 0))
    full = lambda a: pl.BlockSpec(a.shape, lambda b, pt: (0,) * a.ndim)
    in_specs = ([pl.BlockSpec(memory_space=pl.ANY)] * 5
                + [row_spec] * 6
                + [pl.BlockSpec((1, 8, 1), lambda b, pt: (b, 0, 0)), full(bias), full(bias0)]
                + [_layer_block(a, layer) for a in (lq1, lk1, lq2, lk2, g)])
    scratch = [pltpu.VMEM((n_buf, n_slots, width, page), F32), pltpu.VMEM((n_buf, n_slots, width, page), F32),
               pltpu.VMEM((n_buf, n_slots, 8, page), F32), pltpu.SemaphoreType.DMA((n_buf, 3)),
               pltpu.VMEM((n_pages, 8, page), F32), pltpu.VMEM((n_pages, 8, page), F32),
               pltpu.VMEM((n_pages, 8, dv.shape[1]), F32)]
    grid_spec = pltpu.PrefetchScalarGridSpec(
        num_scalar_prefetch=1, grid=(bs,), in_specs=in_specs,
        out_specs=[row_spec, row_spec], scratch_shapes=scratch)
    return pl.pallas_call(
        functools.partial(_decode_kernel, base=base, n_slots=n_slots, n_pages=n_pages, page=page,
                          lam_init=lam_init),
        grid_spec=grid_spec,
        out_shape=[jax.ShapeDtypeStruct((bs, 1, width), F32), jax.ShapeDtypeStruct((bs, 1, width), F32)],
        compiler_params=pltpu.CompilerParams(dimension_semantics=("arbitrary",), vmem_limit_bytes=VMEM_LIMIT),
        name="decode_attention",
    )(page_table, fk, lft, dk, fv, dv, fq, dq, fkn, fvn, dkn, dvn, lfn, bias, bias0, lq1, lk1, lq2, lk2, g)


def kernel(x_prompt, x_sample, cache_fox_k, cache_fox_v, cache_fox_logf, cache_diff_k, cache_diff_v, page_table,
           norm_g, w_in, b_forget, lambda_q1, lambda_k1, lambda_q2, lambda_k2, diff_subln_g, w_branch_fox,
           w_branch_diff, w_out, rel_bias, final_norm_g):
    depth = w_in.shape[0]
    b, s, d = x_prompt.shape
    bs = x_sample.shape[0]
    n_pool, page = cache_fox_k.shape[1], cache_fox_k.shape[2]
    n_pages = page_table.shape[1]
    past_len = n_pages * page
    n_fox = FOX_WIDTH // HEAD_DIM
    n_diff = DIFF_WIDTH // (2 * HEAD_DIM)
    tq = 256
    n_units = 4
    n_slots = 4

    w_t = jnp.swapaxes(w_in, 1, 2)
    w_main = jnp.concatenate([w_t[:, :3 * FOX_WIDTH], w_t[:, 3 * FOX_WIDTH + n_fox:]], axis=1).astype(BF16)
    w_f = jnp.pad(w_t[:, 3 * FOX_WIDTH:3 * FOX_WIDTH + n_fox], ((0, 0), (0, LANES - n_fox), (0, 0))).astype(BF16)
    b_f = jnp.pad(b_forget, ((0, 0), (0, LANES - n_fox)))
    w_bf = w_branch_fox.astype(BF16)
    w_bd = w_branch_diff.astype(BF16)
    w_o = w_out.astype(BF16)

    fk_c = cache_fox_k.transpose(0, 1, 3, 4, 2).reshape(depth * n_pool, FOX_WIDTH, page)
    fv_c = cache_fox_v.transpose(0, 1, 3, 4, 2).reshape(depth * n_pool, FOX_WIDTH, page)
    dk_c = cache_diff_k.transpose(0, 1, 3, 4, 5, 2).reshape(depth * n_pool, DIFF_WIDTH, page)
    dv_c = cache_diff_v.reshape(depth * n_pool, page * n_diff, 2 * HEAD_DIM)
    lf_c = cache_fox_logf.transpose(0, 1, 3, 2).reshape(depth * n_pool, n_fox, page)

    bias_p = _prompt_bias(rel_bias, tq, tq)
    bias_s, bias_s0 = _decode_bias(rel_bias, past_len, page)

    xp = x_prompt
    xs = x_sample.reshape(bs, d)
    stacked = ()
    outs_s = [[] for _ in range(5)]
    g = norm_g[:, None, :]
    lvec = [a[:, None, :] for a in (lambda_q1, lambda_k1, lambda_q2, lambda_k2)]
    for l in range(depth):
        lam_init = _lambda_init(l)
        final = l == depth - 1

        (fqt, fkb, fvt, fz, dqt, dkb, dvt, dz, ga, gb, ct, c, *stacked) = _inproj_prompt(
            xp, g, w_main, w_f, b_f[:, :, None], tuple(stacked), l, depth, tm=tq)
        chunks = lambda a: a.reshape(b, s // tq, a.shape[1], tq)
        rows = lambda a: a.reshape(b, s, -1)
        n_streams = 2 * n_units
        cq = ct.reshape(b, n_fox // n_streams, n_streams, s)
        ck = c.reshape(b, s, n_fox // n_streams, n_streams).transpose(0, 2, 1, 3)
        o_f = _fox_attention(chunks(fqt), rows(fkb), chunks(fvt), cq, ck, tq=tq, n_units=n_units)
        o_d = _diff_attention(chunks(dqt), rows(dkb), chunks(dvt), bias_p, *lvec, diff_subln_g[:, :, None], l,
                              tq=tq, n_units=n_units, lam_init=lam_init)
        xp = _merge(o_f.reshape(b * s, -1), fz, o_d.reshape(b * s, -1), dz, ga, gb, xp.reshape(b * s, d),
                    w_bf, w_bd, w_o, final_norm_g[None], l, tm=2 * tq, final=final,
                    name="merge_prompt").reshape(b, s, d)

        sq, sk, sv, sz, sdq, sdk, sdv, sdz, sga, sgb, slf = _inproj_sample(xs, g, w_main, w_f, b_f[:, None, :], l)
        row3 = lambda a: a.reshape(bs, 1, -1)
        o_fs, o_ds = _decode_attention(
            page_table, l * n_pool, fk_c, lf_c, dk_c, fv_c, dv_c,
            row3(sq), row3(sdq), row3(sk), row3(sv), row3(sdk), row3(sdv), slf.reshape(bs, n_fox, 1),
            bias_s, bias_s0, *lvec, diff_subln_g[:, None, :], l, n_slots=n_slots, lam_init=lam_init)
        xs = _merge(o_fs.reshape(bs, -1), sz, o_ds.reshape(bs, -1), sdz, sga, sgb, xs, w_bf, w_bd, w_o,
                    final_norm_g[None], l, tm=bs, final=final, name="merge_sample")
        for acc, a in zip(outs_s, (sk, sv, slf, sdk, sdv)):
            acc.append(a)

    pk, pv, plf, pdk, pdv = stacked
    sk_l, sv_l, slf_l, sdk_l, sdv_l = [jnp.stack(a) for a in outs_s]
    return (xp, xs.reshape(bs, 1, d),
            pk.reshape(depth, b, n_fox, HEAD_DIM, s).transpose(0, 1, 4, 2, 3),
            pv.reshape(depth, b, n_fox, HEAD_DIM, s).transpose(0, 1, 4, 2, 3),
            plf.transpose(0, 1, 3, 2),
            pdk.reshape(depth, b, n_diff, 2, HEAD_DIM, s).transpose(0, 1, 5, 2, 3, 4),
            pdv.reshape(depth, b, s, n_diff, 2 * HEAD_DIM),
            sk_l.reshape(depth, bs, 1, n_fox, HEAD_DIM), sv_l.reshape(depth, bs, 1, n_fox, HEAD_DIM),
            slf_l.reshape(depth, bs, 1, n_fox),
            sdk_l.reshape(depth, bs, 1, n_diff, 2, HEAD_DIM), sdv_l.reshape(depth, bs, 1, n_diff, 2 * HEAD_DIM))
```
